```python
import math, functools
import jax, jax.numpy as jnp
from jax import lax
import numpy as np

D_MODEL = 1024
BATCH = 4
SEQ = 4096
DEPTH = 1
DEC_BATCH = 128
DEC_SEQ = 1
PAST_LEN = 2048
PAGE_SIZE = 128

N_HEADS = 8
HEAD_DIM = 64
D_ATTN = N_HEADS * HEAD_DIM
D_CONV = D_MODEL - D_ATTN
N_IDX_HEADS = 8
IDX_DIM = 64
TOPK_MAX = 256
CONV_WIDTH = 31
FFN_CONV_WIDTH = 3
D_FF = 2816
N_MEM = 256
MEM_HEADS = 4
MEM_HEAD_DIM = 64
D_MEM_ATTN = MEM_HEADS * MEM_HEAD_DIM
REL_BUCKETS = 32
REL_MAX_DIST = 128
Q_BLOCK = 128
D_IN = 3 * D_ATTN + N_IDX_HEADS * IDX_DIM + IDX_DIM + N_IDX_HEADS + 2 * D_CONV
RMS_EPS = 1e-6
LN_EPS = 1e-5
NEG_INF = -1e30

kernel_name = "hymba_dsa_conformer_convffn_decode_step"


def _in_proj_offsets():
    sizes = (D_ATTN, D_ATTN, D_ATTN, N_IDX_HEADS * IDX_DIM, IDX_DIM, N_IDX_HEADS, 2 * D_CONV)
    offs, acc = [], 0
    for s in sizes[:-1]:
        acc += s
        offs.append(acc)
    return offs


def rmsnorm(x, g):
    xf = x.astype(jnp.float32)
    y = xf * lax.rsqrt(jnp.mean(xf * xf, -1, keepdims=True) + RMS_EPS)
    return (y * g.astype(jnp.float32)).astype(x.dtype)


def layernorm(x, g, b):
    xf = x.astype(jnp.float32)
    mu = jnp.mean(xf, -1, keepdims=True)
    var = jnp.mean(jnp.square(xf - mu), -1, keepdims=True)
    y = (xf - mu) * lax.rsqrt(var + LN_EPS)
    return (y * g.astype(jnp.float32) + b.astype(jnp.float32)).astype(x.dtype)


def rel_bucket(rel):
    n = jnp.maximum(rel, 0)
    max_exact = REL_BUCKETS // 2
    nf = jnp.maximum(n, 1).astype(jnp.float32)
    large = max_exact + (jnp.log(nf / max_exact) / math.log(REL_MAX_DIST / max_exact)
                         * (REL_BUCKETS - max_exact)).astype(jnp.int32)
    large = jnp.minimum(large, REL_BUCKETS - 1)
    return jnp.where(n < max_exact, n, large)


def causal_dwconv(x_ext, w, b):
    c = x_ext.shape[-1]
    y = lax.conv_general_dilated(x_ext, w[:, None, :].astype(x_ext.dtype), window_strides=(1,),
                                 padding='VALID', dimension_numbers=('NWC', 'WIO', 'NWC'),
                                 feature_group_count=c)
    return y + b


def indexer_scores(q_idx, w_idx, k_idx):
    s = jnp.einsum('bthd,bsd->bths', q_idx, k_idx) * (IDX_DIM ** -0.5)
    return jnp.einsum('bths,bth->bts', jax.nn.relu(s), w_idx * (N_IDX_HEADS ** -0.5))


def sparse_attend(q, kg, vg, q_pos, key_pos, rel_table):
    rel = q_pos[None, :, None] - key_pos
    bias = jnp.swapaxes(rel_table[rel_bucket(rel)], -1, -2)
    logits = (jnp.einsum('bthd,btkhd->bthk', q, kg).astype(jnp.float32) * (HEAD_DIM ** -0.5)
              + bias.astype(jnp.float32))
    logits = jnp.where((rel >= 0)[:, :, None, :], logits, NEG_INF)
    p = jax.nn.softmax(logits, axis=-1).astype(vg.dtype)
    return jnp.einsum('bthk,btkhd->bthd', p, vg)


def dsa_prompt(rel_table, q, k, v, q_idx, k_idx, w_idx):
    b, s, h, dh = q.shape
    topk = min(TOPK_MAX, s // 4)
    nb = s // Q_BLOCK
    key_pos = jnp.arange(s)
    bidx = jnp.arange(b)[:, None, None]

    def blockify(a):
        return jnp.moveaxis(a.reshape((b, nb, Q_BLOCK) + a.shape[2:]), 1, 0)

    def one_block(args):
        blk, qb, qib, wb = args
        q_pos = blk * Q_BLOCK + jnp.arange(Q_BLOCK)
        sc = indexer_scores(qib, wb, k_idx)
        sc = jnp.where(key_pos[None, None, :] <= q_pos[None, :, None], sc, -jnp.inf)
        _, idx = lax.top_k(sc, topk)
        return sparse_attend(qb, k[bidx, idx], v[bidx, idx], q_pos, idx, rel_table)

    out = lax.map(one_block, (jnp.arange(nb), blockify(q), blockify(q_idx), blockify(w_idx)))
    return jnp.moveaxis(out, 0, 1).reshape(b, s, h * dh)


def dsa_sample(cache_k, cache_v, cache_idx_k, page_table, rel_table, q, k, v, q_idx, k_idx, w_idx):
    db, t, h, dh = q.shape
    n_pages = page_table.shape[1]
    ps = cache_k.shape[1]
    past = n_pages * ps
    l_keys = past + t
    topk = min(TOPK_MAX, l_keys // 4)
    k_idx_past = cache_idx_k[page_table].reshape(db, past, IDX_DIM)
    k_idx_all = jnp.concatenate([k_idx_past.astype(k_idx.dtype), k_idx], axis=1)
    q_pos = past + jnp.arange(t)
    sc = indexer_scores(q_idx, w_idx, k_idx_all)
    sc = jnp.where(jnp.arange(l_keys)[None, None, :] <= q_pos[None, :, None], sc, -jnp.inf)
    _, idx = lax.top_k(sc, topk)
    bidx = jnp.arange(db)[:, None, None]
    pidx = jnp.minimum(idx, past - 1)
    phys = page_table[bidx, pidx // ps]
    off = pidx % ps
    nidx = jnp.clip(idx - past, 0, t - 1)
    sel = (idx < past)[..., None, None]
    kg = jnp.where(sel, cache_k[phys, off].astype(k.dtype), k[bidx, nidx])
    vg = jnp.where(sel, cache_v[phys, off].astype(v.dtype), v[bidx, nidx])
    return sparse_attend(q, kg, vg, q_pos, idx, rel_table).reshape(db, t, h * dh)


def mem_kv(mem, g_mem, w_ckv):
    b, n, _ = mem.shape
    km, vm = jnp.split(rmsnorm(mem, g_mem) @ w_ckv, 2, axis=-1)
    return km.reshape(b, n, MEM_HEADS, MEM_HEAD_DIM), vm.reshape(b, n, MEM_HEADS, MEM_HEAD_DIM)


def mem_attend(h, w_cq, km, vm, w_co):
    b, t, _ = h.shape
    q = (h @ w_cq).reshape(b, t, MEM_HEADS, MEM_HEAD_DIM)
    logits = jnp.einsum('bthd,bnhd->bhtn', q, km.astype(q.dtype)).astype(jnp.float32) * (MEM_HEAD_DIM ** -0.5)
    p = jax.nn.softmax(logits, axis=-1).astype(q.dtype)
    o = jnp.einsum('bhtn,bnhd->bthd', p, vm.astype(q.dtype)).reshape(b, t, D_MEM_ATTN)
    return o @ w_co


def _layer(x, attn_fn, conv_prev, ffn_prev, mem_k, mem_v, lw):
    b, t, _ = x.shape
    h = rmsnorm(x, lw['norm_mix'])
    q, k, v, qi, ki, wi, glu = jnp.split(h @ lw['w_in'], _in_proj_offsets(), axis=-1)
    q = q.reshape(b, t, N_HEADS, HEAD_DIM)
    k = k.reshape(b, t, N_HEADS, HEAD_DIM)
    v = v.reshape(b, t, N_HEADS, HEAD_DIM)
    qi = qi.reshape(b, t, N_IDX_HEADS, IDX_DIM)
    a = attn_fn(q, k, v, qi, ki, wi)
    u = glu[..., :D_CONV] * jax.nn.sigmoid(glu[..., D_CONV:])
    u_ext = jnp.concatenate([conv_prev.astype(u.dtype), u], axis=1)
    c = jax.nn.silu(layernorm(causal_dwconv(u_ext, lw['conv_dw'], lw['conv_db']),
                              lw['conv_ln_g'], lw['conv_ln_b']))
    x = x + jnp.concatenate([a, c], axis=-1) @ lw['w_o']
    x = x + mem_attend(rmsnorm(x, lw['norm_mem_q']), lw['w_cq'], mem_k, mem_v, lw['w_co'])
    up = rmsnorm(x, lw['norm_ffn']) @ lw['w_up']
    up_ext = jnp.concatenate([ffn_prev.astype(up.dtype), up], axis=1)
    g, val = jnp.split(causal_dwconv(up_ext, lw['ffn_dw'], lw['ffn_db']), 2, axis=-1)
    x = x + (jax.nn.silu(g) * val) @ lw['w_down']
    return x, k, v, ki, u_ext[:, -(CONV_WIDTH - 1):], up_ext[:, -(FFN_CONV_WIDTH - 1):]


def setup_inputs(seed: int = 0) -> dict:
    key = jax.random.key(seed)
    ks = jax.random.split(key, 32)
    f32 = jnp.float32

    def nrm(k, shape, scale):
        return jax.random.normal(k, shape, f32) * scale

    n_pages = PAST_LEN // PAGE_SIZE
    n_used = DEC_BATCH * n_pages
    n_pool = (5 * n_used + 3) // 4
    page_table = jax.random.permutation(ks[0], n_pool)[:n_used].reshape(DEC_BATCH, n_pages).astype(jnp.int32)
    return {
        'x_prompt': nrm(ks[1], (BATCH, SEQ, D_MODEL), 1.0),
        'x_sample': nrm(ks[2], (DEC_BATCH, DEC_SEQ, D_MODEL), 1.0),
        'mem_prompt': nrm(ks[3], (BATCH, N_MEM, D_MODEL), 1.0),
        'cache_k': nrm(ks[4], (DEPTH, n_pool, PAGE_SIZE, N_HEADS, HEAD_DIM), 1.0),
        'cache_v': nrm(ks[5], (DEPTH, n_pool, PAGE_SIZE, N_HEADS, HEAD_DIM), 1.0),
        'cache_idx_k': nrm(ks[6], (DEPTH, n_pool, PAGE_SIZE, IDX_DIM), 1.0),
        'page_table': page_table,
        'cache_conv': nrm(ks[7], (DEPTH, DEC_BATCH, CONV_WIDTH - 1, D_CONV), 0.5),
        'cache_ffn': nrm(ks[8], (DEPTH, DEC_BATCH, FFN_CONV_WIDTH - 1, 2 * D_FF), 1.0),
        'cache_mem_k': nrm(ks[9], (DEPTH, DEC_BATCH, N_MEM, MEM_HEADS, MEM_HEAD_DIM), 1.0),
        'cache_mem_v': nrm(ks[10], (DEPTH, DEC_BATCH, N_MEM, MEM_HEADS, MEM_HEAD_DIM), 1.0),
        'rel_bias': nrm(ks[11], (REL_BUCKETS, N_HEADS), 0.5),
        'norm_mix': 1.0 + nrm(ks[12], (DEPTH, D_MODEL), 0.02),
        'w_in': nrm(ks[13], (DEPTH, D_MODEL, D_IN), D_MODEL ** -0.5),
        'conv_dw': nrm(ks[14], (DEPTH, CONV_WIDTH, D_CONV), CONV_WIDTH ** -0.5),
        'conv_db': nrm(ks[15], (DEPTH, D_CONV), 0.02),
        'conv_ln_g': 1.0 + nrm(ks[16], (DEPTH, D_CONV), 0.02),
        'conv_ln_b': nrm(ks[17], (DEPTH, D_CONV), 0.02),
        'w_o': nrm(ks[18], (DEPTH, D_MODEL, D_MODEL), D_MODEL ** -0.5),
        'norm_mem_q': 1.0 + nrm(ks[19], (DEPTH, D_MODEL), 0.02),
        'norm_mem_kv': 1.0 + nrm(ks[20], (DEPTH, D_MODEL), 0.02),
        'w_cq': nrm(ks[21], (DEPTH, D_MODEL, D_MEM_ATTN), D_MODEL ** -0.5),
        'w_ckv': nrm(ks[22], (DEPTH, D_MODEL, 2 * D_MEM_ATTN), D_MODEL ** -0.5),
        'w_co': nrm(ks[23], (DEPTH, D_MEM_ATTN, D_MODEL), D_MEM_ATTN ** -0.5),
        'norm_ffn': 1.0 + nrm(ks[24], (DEPTH, D_MODEL), 0.02),
        'w_up': nrm(ks[25], (DEPTH, D_MODEL, 2 * D_FF), D_MODEL ** -0.5),
        'ffn_dw': nrm(ks[26], (DEPTH, FFN_CONV_WIDTH, 2 * D_FF), FFN_CONV_WIDTH ** -0.5),
        'ffn_db': nrm(ks[27], (DEPTH, 2 * D_FF), 0.02),
        'w_down': nrm(ks[28], (DEPTH, D_FF, D_MODEL), D_FF ** -0.5),
        'norm_final': 1.0 + nrm(ks[29], (D_MODEL,), 0.02),
    }


def reference(x_prompt, x_sample, mem_prompt, cache_k, cache_v, cache_idx_k, page_table, cache_conv,
              cache_ffn, cache_mem_k, cache_mem_v, rel_bias, norm_mix, w_in, conv_dw, conv_db, conv_ln_g,
              conv_ln_b, w_o, norm_mem_q, norm_mem_kv, w_cq, w_ckv, w_co, norm_ffn, w_up, ffn_dw, ffn_db,
              w_down, norm_final):
    yp, ys = x_prompt, x_sample
    bp = x_prompt.shape[0]
    kp_l, vp_l, kip_l, cp_l, fp_l, mkp_l, mvp_l = [], [], [], [], [], [], []
    ks_l, vs_l, kis_l, cs_l, fs_l = [], [], [], [], []
    for l in range(DEPTH):
        lw = dict(norm_mix=norm_mix[l], w_in=w_in[l], conv_dw=conv_dw[l], conv_db=conv_db[l],
                  conv_ln_g=conv_ln_g[l], conv_ln_b=conv_ln_b[l], w_o=w_o[l], norm_mem_q=norm_mem_q[l],
                  w_cq=w_cq[l], w_co=w_co[l], norm_ffn=norm_ffn[l], w_up=w_up[l], ffn_dw=ffn_dw[l],
                  ffn_db=ffn_db[l], w_down=w_down[l])
        mkp, mvp = mem_kv(mem_prompt, norm_mem_kv[l], w_ckv[l])
        conv0 = jnp.zeros((bp, CONV_WIDTH - 1, D_CONV), yp.dtype)
        ffn0 = jnp.zeros((bp, FFN_CONV_WIDTH - 1, 2 * D_FF), yp.dtype)
        yp, kp, vp, kip, cp, fp = _layer(yp, functools.partial(dsa_prompt, rel_bias), conv0, ffn0,
                                         mkp, mvp, lw)
        attn_s = functools.partial(dsa_sample, cache_k[l], cache_v[l], cache_idx_k[l], page_table, rel_bias)
        ys, k_s, v_s, ki_s, c_s, f_s = _layer(ys, attn_s, cache_conv[l], cache_ffn[l],
                                             cache_mem_k[l], cache_mem_v[l], lw)
        kp_l.append(kp); vp_l.append(vp); kip_l.append(kip); cp_l.append(cp); fp_l.append(fp)
        mkp_l.append(mkp); mvp_l.append(mvp)
        ks_l.append(k_s); vs_l.append(v_s); kis_l.append(ki_s); cs_l.append(c_s); fs_l.append(f_s)
    y_prompt = rmsnorm(yp, norm_final)
    y_sample = rmsnorm(ys, norm_final)
    k_prompt = jnp.stack(kp_l); v_prompt = jnp.stack(vp_l); idx_k_prompt = jnp.stack(kip_l)
    k_sample = jnp.stack(ks_l); v_sample = jnp.stack(vs_l); idx_k_sample = jnp.stack(kis_l)
    conv_prompt = jnp.stack(cp_l); conv_sample = jnp.stack(cs_l)
    ffn_prompt = jnp.stack(fp_l); ffn_sample = jnp.stack(fs_l)
    mem_k_prompt = jnp.stack(mkp_l); mem_v_prompt = jnp.stack(mvp_l)
    return (y_prompt, y_sample, k_prompt, v_prompt, idx_k_prompt, k_sample, v_sample, idx_k_sample,
            conv_prompt, conv_sample, ffn_prompt, ffn_sample, mem_k_prompt, mem_v_prompt)
```

```python
import functools
import math

import jax
import jax.numpy as jnp
from jax import lax
from jax.experimental import pallas as pl
from jax.experimental.pallas import tpu as pltpu

F32, BF16, I32 = jnp.float32, jnp.bfloat16, jnp.int32

N_HEADS = 8
HEAD_DIM = 64
D_ATTN = N_HEADS * HEAD_DIM
N_IDX_HEADS = 8
IDX_DIM = 64
TOPK_MAX = 256
CONV_WIDTH = 31
FFN_CONV_WIDTH = 3
MEM_HEADS = 4
MEM_HEAD_DIM = 64
D_MEM_ATTN = MEM_HEADS * MEM_HEAD_DIM
REL_BUCKETS = 32
REL_MAX_DIST = 128
RMS_EPS = 1e-6
LN_EPS = 1e-5
NEG_INF = -1e30
INT_MIN = -(2 ** 31)
LANES = 128
SUBLANES = 8
VMEM_LIMIT_BYTES = 56 * 1024 * 1024
CONV_HALO = 32


def _cparams(sem):
    return pltpu.CompilerParams(dimension_semantics=sem, vmem_limit_bytes=VMEM_LIMIT_BYTES)


def _dot(a, b):
    return jnp.dot(a, b, preferred_element_type=F32)


def _dot_nt(a, b):
    return lax.dot_general(a, b, (((1,), (1,)), ((), ())), preferred_element_type=F32)


def _rms(x, g):
    return x * lax.rsqrt(jnp.mean(x * x, -1, keepdims=True) + RMS_EPS) * g


def _sigmoid(x):
    return 1.0 / (1.0 + jnp.exp(-x))


def _sort_key(x):
    b = lax.bitcast_convert_type(x, I32)
    b = jnp.where(b == INT_MIN, 0, b)
    return b ^ ((b >> 31) & 0x7FFFFFFF)


def _rel_bucket(n):
    n = jnp.maximum(n, 0)
    max_exact = REL_BUCKETS // 2
    nf = jnp.maximum(n, 1).astype(F32)
    large = max_exact + (jnp.log(nf / max_exact) / math.log(REL_MAX_DIST / max_exact)
                         * (REL_BUCKETS - max_exact)).astype(I32)
    large = jnp.minimum(large, REL_BUCKETS - 1)
    return jnp.where(n < max_exact, n, large)


def _split_bf16(x):
    hi = x.astype(BF16)
    lo = (x - hi.astype(F32)).astype(BF16)
    return hi, lo


def _head_select(n_rows, n_cols, rows_per_head, by_row):
    r = lax.broadcasted_iota(I32, (n_rows, n_cols), 0)
    c = lax.broadcasted_iota(I32, (n_rows, n_cols), 1)
    hit = (r // rows_per_head == c) if by_row else (c // rows_per_head == r)
    return jnp.where(hit, 1.0, 0.0).astype(BF16)


def _inproj_kernel(x_ref, g_ref, w_ref, k_ref, v_ref, kiwi_ref, qbf_ref, kbf_ref, vt_ref,
                   qibf_ref, kibf_ref, u_ref):
    hb = _rms(x_ref[...], g_ref[...]).astype(BF16)

    def mm(lo, hi):
        return _dot(hb, w_ref[:, lo:hi])

    qbf_ref[...] = (mm(0, 512) * (HEAD_DIM ** -0.5)).astype(BF16)
    k = mm(512, 1024)
    k_ref[...] = k
    kbf_ref[...] = k.astype(BF16)
    v = mm(1024, 1536)
    v_ref[...] = v
    vt_ref[...] = v.T.astype(BF16)
    qibf_ref[...] = (mm(1536, 2048) * (IDX_DIM ** -0.5)).astype(BF16)
    u_ref[...] = mm(2048, 2560) * _sigmoid(mm(2560, 3072))
    kw = mm(3072, 3200)
    kiwi_ref[...] = kw
    kibf_ref[...] = kw[:, :IDX_DIM].astype(BF16)


def _inproj(x, g, w_cat, tm):
    b, s, d = x.shape
    n_w = w_cat.shape[1]
    row = lambda width: pl.BlockSpec((None, tm, width), lambda bi, si: (bi, si, 0))
    const = lambda shape: pl.BlockSpec(shape, lambda bi, si: (0,) * len(shape))
    out_shape = (
        jax.ShapeDtypeStruct((b, s, D_ATTN), F32),
        jax.ShapeDtypeStruct((b, s, D_ATTN), F32),
        jax.ShapeDtypeStruct((b, s, LANES), F32),
        jax.ShapeDtypeStruct((b, s, D_ATTN), BF16),
        jax.ShapeDtypeStruct((b, s, D_ATTN), BF16),
        jax.ShapeDtypeStruct((b, D_ATTN, s), BF16),
        jax.ShapeDtypeStruct((b, s, D_ATTN), BF16),
        jax.ShapeDtypeStruct((b, s, IDX_DIM), BF16),
        jax.ShapeDtypeStruct((b, s, D_ATTN), F32),
    )
    out_specs = (row(D_ATTN), row(D_ATTN), row(LANES), row(D_ATTN), row(D_ATTN),
                 pl.BlockSpec((None, D_ATTN, tm), lambda bi, si: (bi, 0, si)),
                 row(D_ATTN), row(IDX_DIM), row(D_ATTN))
    return pl.pallas_call(
        _inproj_kernel, grid=(b, s // tm),
        in_specs=[row(d), const((1, d)), const((d, n_w))],
        out_specs=out_specs, out_shape=out_shape,
        compiler_params=_cparams(("parallel", "parallel")), name="inproj",
    )(x, g, w_cat)


def _conv_ln_silu(acc, g, b):
    mu = jnp.mean(acc, -1, keepdims=True)
    xc = acc - mu
    var = jnp.mean(xc * xc, -1, keepdims=True)
    y = xc * lax.rsqrt(var + LN_EPS) * g + b
    return y * _sigmoid(y)


def _conv_prompt_kernel(u_ref, halo_ref, w_ref, b_ref, g_ref, bb_ref, c_ref, ext_sc, *, tm, rc):
    i = pl.program_id(1)
    ext_sc[0:CONV_HALO, :] = jnp.where(i == 0, 0.0, halo_ref[...])
    ext_sc[CONV_HALO:CONV_HALO + tm, :] = u_ref[...]
    first = CONV_HALO - (CONV_WIDTH - 1)
    for r in range(tm // rc):
        acc = jnp.zeros((rc, u_ref.shape[-1]), F32) + b_ref[...]
        for j in range(CONV_WIDTH):
            lo = r * rc + first + j
            acc = acc + w_ref[j:j + 1, :] * ext_sc[lo:lo + rc, :]
        c_ref[r * rc:(r + 1) * rc, :] = _conv_ln_silu(acc, g_ref[...], bb_ref[...]).astype(BF16)


def _conv_prompt(u, w, b, g, bb, tm):
    bsz, s, c = u.shape
    hb = tm // CONV_HALO
    const = lambda shape: pl.BlockSpec(shape, lambda bi, si: (0,) * len(shape))
    return pl.pallas_call(
        functools.partial(_conv_prompt_kernel, tm=tm, rc=64), grid=(bsz, s // tm),
        in_specs=[pl.BlockSpec((None, tm, c), lambda bi, si: (bi, si, 0)),
                  pl.BlockSpec((None, CONV_HALO, c), lambda bi, si: (bi, jnp.maximum(si * hb - 1, 0), 0)),
                  const((CONV_WIDTH, c)), const((1, c)), const((1, c)), const((1, c))],
        out_specs=pl.BlockSpec((None, tm, c), lambda bi, si: (bi, si, 0)),
        out_shape=jax.ShapeDtypeStruct((bsz, s, c), BF16),
        scratch_shapes=[pltpu.VMEM((tm + CONV_HALO, c), F32)],
        compiler_params=_cparams(("parallel", "parallel")), name="conv_prompt",
    )(u, u, w, b, g, bb)


def _conv_sample_kernel(hist_ref, u_ref, w_ref, b_ref, g_ref, bb_ref, c_ref):
    acc = w_ref[CONV_WIDTH - 1:CONV_WIDTH, :] * u_ref[...] + b_ref[...]
    for j in range(CONV_WIDTH - 1):
        acc = acc + w_ref[j:j + 1, :] * hist_ref[j]
    c_ref[...] = _conv_ln_silu(acc, g_ref[...], bb_ref[...]).astype(BF16)


def _conv_sample(hist_t, u, w, b, g, bb):
    return pl.pallas_call(
        _conv_sample_kernel, out_shape=jax.ShapeDtypeStruct(u.shape, BF16),
        compiler_params=pltpu.CompilerParams(vmem_limit_bytes=VMEM_LIMIT_BYTES), name="conv_sample",
    )(hist_t, u, w, b, g, bb)


def _dsa_prompt_kernel(tab_ref, q_ref, qi_ref, kiwi_ref, ki_ref, k_ref, vt_ref, a_ref,
                       keys_sc, qz_sc, qit_sc, wt_sc, bias_sc, m_sc, l_sc, acc_sc, *, tq, topk):
    kc = tq
    b = pl.program_id(0)
    i = pl.program_id(1)

    @pl.when((b == 0) & (i == 0))
    def _build_bias():
        kk = lax.broadcasted_iota(I32, (kc, tq), 0)
        qq = lax.broadcasted_iota(I32, (kc, tq), 1)
        for d in range(2):
            n = d * tq + qq - kk
            bucket = _rel_bucket(n)
            for h in range(N_HEADS):
                val = jnp.zeros((kc, tq), F32)
                for bk in range(REL_BUCKETS):
                    val = jnp.where(bucket == bk, tab_ref[bk, h], val)
                if d == 0:
                    val = jnp.where(n >= 0, val, NEG_INF)
                bias_sc[d, h] = val

    qt = q_ref[...].astype(F32).T
    zeros = jnp.zeros((HEAD_DIM, tq), BF16)
    for h in range(N_HEADS):
        blk = qt[h * HEAD_DIM:(h + 1) * HEAD_DIM, :].astype(BF16)
        qz_sc[h] = jnp.concatenate([blk, zeros] if h % 2 == 0 else [zeros, blk], axis=0)
    qit_sc[...] = qi_ref[...].astype(F32).T.astype(BF16)
    wt_sc[...] = kiwi_ref[...].T[IDX_DIM:IDX_DIM + N_IDX_HEADS, :] * (N_IDX_HEADS ** -0.5)

    nc = i + 1
    qpos = i * tq + lax.broadcasted_iota(I32, (kc, tq), 1)

    def chunk(c):
        return pl.ds(pl.multiple_of(c * kc, kc), kc)

    def score_body(c, carry):
        kchunk = ki_ref[chunk(c), :]
        acc = jnp.zeros((kc, tq), F32)
        for h in range(N_IDX_HEADS):
            s = _dot(kchunk, qit_sc[h * IDX_DIM:(h + 1) * IDX_DIM, :])
            acc = acc + wt_sc[h:h + 1, :] * jnp.maximum(s, 0.0)
        kpos = c * kc + lax.broadcasted_iota(I32, (kc, tq), 0)
        keys_sc[chunk(c), :] = jnp.where(kpos <= qpos, _sort_key(acc), INT_MIN)
        return carry

    lax.fori_loop(0, nc, score_body, 0)

    def count(pred):
        def body(c, cnt):
            ind = jnp.where(pred(keys_sc[chunk(c), :]), 1, 0).astype(I32)
            return cnt + ind.reshape(kc // SUBLANES, SUBLANES, tq).sum(axis=0)
        cnt8 = lax.fori_loop(0, nc, body, jnp.zeros((SUBLANES, tq), I32))
        return jnp.sum(cnt8, axis=0, keepdims=True)

    def bit_body(it, t):
        tc = t | jnp.left_shift(jnp.int32(1), 31 - it)
        tcs = tc ^ INT_MIN
        return jnp.where(count(lambda k: k >= tcs) >= topk, tc, t)

    thr = lax.fori_loop(0, 32, bit_body, jnp.zeros((1, tq), I32)) ^ INT_MIN

    tie = (count(lambda k: k >= thr) > topk) & (thr > INT_MIN)

    @pl.when(jnp.max(tie.astype(I32)) > 0)
    def _resolve_ties():
        need = (topk - count(lambda k: k > thr)).astype(F32)
        r = lax.broadcasted_iota(I32, (kc, kc), 0)
        cc = lax.broadcasted_iota(I32, (kc, kc), 1)
        lower = jnp.where(r >= cc, 1.0, 0.0).astype(BF16)

        def fix_body(c, seen):
            k = keys_sc[chunk(c), :]
            eq = k == thr
            eqf = jnp.where(eq, 1.0, 0.0)
            rank = _dot(lower, eqf.astype(BF16)) + seen
            drop = eq & (rank > need) & tie
            keys_sc[chunk(c), :] = jnp.where(drop, k - 1, k)
            return seen + jnp.sum(eqf, axis=0, keepdims=True)

        lax.fori_loop(0, nc, fix_body, jnp.zeros((1, tq), F32))

    m_sc[...] = jnp.full(m_sc.shape, -3e38, F32)
    l_sc[...] = jnp.zeros(l_sc.shape, F32)
    acc_sc[...] = jnp.zeros(acc_sc.shape, F32)

    def attend(c, bias_of):
        sel = keys_sc[chunk(c), :] >= thr
        for h in range(N_HEADS):
            hs = slice(h * HEAD_DIM, (h + 1) * HEAD_DIM)
            pair = (h // 2) * 2 * HEAD_DIM
            s = _dot(k_ref[chunk(c), pair:pair + 2 * HEAD_DIM], qz_sc[h]) + bias_of(h)
            s = jnp.where(sel, s, NEG_INF)
            m_old = m_sc[h:h + 1, :]
            m_new = jnp.maximum(m_old, jnp.max(s, axis=0, keepdims=True))
            alpha = jnp.exp(m_old - m_new)
            p = jnp.exp(s - m_new)
            l_sc[h:h + 1, :] = alpha * l_sc[h:h + 1, :] + jnp.sum(p, axis=0, keepdims=True)
            acc_sc[hs, :] = alpha * acc_sc[hs, :] + _dot(vt_ref[hs, chunk(c)], p.astype(BF16))
            m_sc[h:h + 1, :] = m_new

    def far_body(c, carry):
        attend(c, lambda h: tab_ref[REL_BUCKETS - 1, h])
        return carry

    lax.fori_loop(0, jnp.maximum(i - 1, 0), far_body, 0)

    @pl.when(i >= 1)
    def _near():
        attend(i - 1, lambda h: bias_sc[1, h])

    attend(i, lambda h: bias_sc[0, h])

    inv = 1.0 / l_sc[...]
    for h in range(N_HEADS):
        hs = slice(h * HEAD_DIM, (h + 1) * HEAD_DIM)
        acc_sc[hs, :] = acc_sc[hs, :] * inv[h:h + 1, :]
    a_ref[...] = acc_sc[...].T.astype(BF16)


def _dsa_prompt(rel_table, q_bf, qi_bf, kiwi, ki_bf, k_bf, vt_bf, tq):
    b, s, _ = q_bf.shape
    assert s % tq == 0 and tq >= REL_MAX_DIST
    topk = min(TOPK_MAX, s // 4)
    tile = lambda width: pl.BlockSpec((None, tq, width), lambda bi, si: (bi, si, 0))
    whole = lambda rows, cols: pl.BlockSpec((None, rows, cols), lambda bi, si: (bi, 0, 0))
    return pl.pallas_call(
        functools.partial(_dsa_prompt_kernel, tq=tq, topk=topk), grid=(b, s // tq),
        in_specs=[pl.BlockSpec(memory_space=pltpu.SMEM), tile(D_ATTN), tile(D_ATTN), tile(LANES),
                  whole(s, IDX_DIM), whole(s, D_ATTN), whole(D_ATTN, s)],
        out_specs=tile(D_ATTN), out_shape=jax.ShapeDtypeStruct((b, s, D_ATTN), BF16),
        scratch_shapes=[
            pltpu.VMEM((s, tq), I32),
            pltpu.VMEM((N_HEADS, 2 * HEAD_DIM, tq), BF16),
            pltpu.VMEM((D_ATTN, tq), BF16),
            pltpu.VMEM((N_IDX_HEADS, tq), F32),
            pltpu.VMEM((2, N_HEADS, tq, tq), F32),
            pltpu.VMEM((N_HEADS, tq), F32), pltpu.VMEM((N_HEADS, tq), F32),
            pltpu.VMEM((D_ATTN, tq), F32),
        ],
        compiler_params=_cparams(("arbitrary", "arbitrary")), name="dsa_prompt",
    )(rel_table, q_bf, qi_bf, kiwi, ki_bf, k_bf, vt_bf)


def _dsa_sample_scores_kernel(pt_ref, qi_ref, w_ref, kinew_ref, *rest, n_pages):
    pages, (sc_ref, snew_ref) = rest[:n_pages], rest[n_pages:]
    qh = qi_ref[...]
    w = w_ref[...] * (N_IDX_HEADS ** -0.5)
    qb = qh.astype(BF16)
    for p in range(n_pages):
        s = _dot_nt(qb, pages[p][...].astype(BF16))
        sc_ref[p:p + 1, :] = jnp.sum(w * jnp.maximum(s, 0.0), axis=0, keepdims=True)
    kn = kinew_ref[...].astype(BF16).astype(F32)
    sn = jnp.sum(qh * kn, axis=1, keepdims=True)
    tot = jnp.sum(w * jnp.maximum(sn, 0.0), axis=0, keepdims=True)
    snew_ref[...] = jnp.broadcast_to(tot, snew_ref.shape)


def _dsa_sample_scores(page_table, qi, w, ki_new, cache_idx_k):
    db, n_pages = page_table.shape
    _, ps, di = cache_idx_k.shape
    page_spec = lambda p: pl.BlockSpec((None, ps, di), lambda bi, pt: (pt[bi * n_pages + p], 0, 0))
    per_b = lambda r, c: pl.BlockSpec((None, r, c), lambda bi, pt: (bi, 0, 0))
    grid_spec = pltpu.PrefetchScalarGridSpec(
        num_scalar_prefetch=1, grid=(db,),
        in_specs=[per_b(N_IDX_HEADS, di), per_b(N_IDX_HEADS, 1), per_b(1, di)] + [page_spec(p) for p in range(n_pages)],
        out_specs=(per_b(n_pages, ps), per_b(1, LANES)))
    return pl.pallas_call(
        functools.partial(_dsa_sample_scores_kernel, n_pages=n_pages), grid_spec=grid_spec,
        out_shape=(jax.ShapeDtypeStruct((db, n_pages, ps), F32), jax.ShapeDtypeStruct((db, 1, LANES), F32)),
        compiler_params=_cparams(("arbitrary",)), name="dsa_sample_scores",
    )(page_table.reshape(-1), qi, w, ki_new, *([cache_idx_k] * n_pages))


def _dsa_sample_select_kernel(sc_ref, snew_ref, sel_ref, selnew_ref, *, topk):
    keys = _sort_key(sc_ref[...])
    knew = _sort_key(snew_ref[...])[:, 0:1]
    rows, past = keys.shape

    def count(kmat, kn):
        return (jnp.sum(jnp.where(kmat, 1, 0).astype(I32), axis=1, keepdims=True)
                + jnp.where(kn, 1, 0).astype(I32))

    def bit_body(it, t):
        tc = t | jnp.left_shift(jnp.int32(1), 31 - it)
        tcs = tc ^ INT_MIN
        return jnp.where(count(keys >= tcs, knew >= tcs) >= topk, tc, t)

    thr = lax.fori_loop(0, 32, bit_body, jnp.zeros((rows, 1), I32)) ^ INT_MIN
    gt = keys > thr
    eq = keys == thr
    need = (topk - count(gt, knew > thr)).astype(F32)
    r = lax.broadcasted_iota(I32, (LANES, LANES), 0)
    c = lax.broadcasted_iota(I32, (LANES, LANES), 1)
    upper = jnp.where(r <= c, 1.0, 0.0).astype(BF16)
    seen = jnp.zeros((rows, 1), F32)
    for j in range(past // LANES):
        sl = slice(j * LANES, (j + 1) * LANES)
        eqf = jnp.where(eq[:, sl], 1.0, 0.0)
        rank = _dot(eqf.astype(BF16), upper) + seen
        sel_ref[:, sl] = jnp.where(gt[:, sl] | (eq[:, sl] & (rank <= need)), 1.0, 0.0)
        seen = seen + jnp.sum(eqf, axis=1, keepdims=True)
    keep_new = (knew > thr) | ((knew == thr) & (seen + 1.0 <= need))
    selnew_ref[...] = jnp.broadcast_to(jnp.where(keep_new, 1.0, 0.0), selnew_ref.shape)


def _dsa_sample_select(scores, snew):
    db, past = scores.shape
    topk = min(TOPK_MAX, (past + 1) // 4)
    return pl.pallas_call(
        functools.partial(_dsa_sample_select_kernel, topk=topk),
        out_shape=(jax.ShapeDtypeStruct((db, past), F32), jax.ShapeDtypeStruct((db, LANES), F32)),
        compiler_params=pltpu.CompilerParams(vmem_limit_bytes=VMEM_LIMIT_BYTES), name="dsa_sample_select",
    )(scores, snew)


def _dsa_sample_attend_kernel(pt_ref, tab_ref, q_ref, knew_ref, vnew_ref, sel_ref, selnew_ref, *rest,
                              n_pages, ps):
    kpages, vpages = rest[:n_pages], rest[n_pages:2 * n_pages]
    a_ref, bias_sc, lg_sc = rest[2 * n_pages:]
    past = n_pages * ps
    lane = lax.broadcasted_iota(I32, (1, LANES), 1)

    def table_row(bk):
        row = jnp.zeros((1, LANES), F32)
        for h in range(N_HEADS):
            row = jnp.where(lane == h, tab_ref[bk, h], row)
        return row

    @pl.when(pl.program_id(0) == 0)
    def _build_bias():
        pos = lax.broadcasted_iota(I32, (past, LANES), 0)
        bucket = _rel_bucket(past - pos)
        val = jnp.zeros((past, LANES), F32)
        for bk in range(REL_BUCKETS):
            val = jnp.where(bucket == bk, table_row(bk), val)
        bias_sc[...] = val

    seg = _head_select(D_ATTN, LANES, HEAD_DIM, by_row=True)
    expand = _head_select(LANES, D_ATTN, HEAD_DIM, by_row=False)
    q = q_ref[...]

    def head_logits(kmat):
        hi, lo = _split_bf16(kmat.astype(BF16).astype(F32) * q)
        return _dot(hi, seg) + _dot(lo, seg)

    sel_t = jnp.concatenate([sel_ref[...], jnp.zeros((LANES - n_pages, ps), F32)], axis=0).T
    for p in range(n_pages):
        lg = head_logits(kpages[p][...]) + bias_sc[p * ps:(p + 1) * ps, :]
        lg_sc[p * ps:(p + 1) * ps, :] = jnp.where(sel_t[:, p:p + 1] > 0.0, lg, NEG_INF)
    lg_new = head_logits(jnp.broadcast_to(knew_ref[...], (SUBLANES, D_ATTN)))[0:1, :] + table_row(0)
    lg_new = jnp.where(selnew_ref[...] > 0.0, lg_new, NEG_INF)

    lg = lg_sc[...]
    m = jnp.maximum(jnp.max(lg, axis=0, keepdims=True), lg_new)
    e = jnp.exp(lg - m)
    e_new = jnp.exp(lg_new - m)
    inv = 1.0 / (jnp.sum(e, axis=0, keepdims=True) + e_new)
    lg_sc[...] = e * inv
    p_new = jnp.broadcast_to((e_new * inv).astype(BF16), (SUBLANES, LANES))
    out = _dot(p_new, expand)[0:1, :] * vnew_ref[...]
    acc = jnp.zeros((SUBLANES, D_ATTN), F32)
    for p in range(n_pages):
        pe = _dot(lg_sc[p * ps:(p + 1) * ps, :].astype(BF16), expand)
        acc = acc + (pe * vpages[p][...]).reshape(ps // SUBLANES, SUBLANES, D_ATTN).sum(axis=0)
    a_ref[...] = out + jnp.sum(acc, axis=0, keepdims=True)


def _dsa_sample_attend(page_table, rel_table, q, k_new, v_new, sel, sel_new, cache_k, cache_v):
    db, n_pages = page_table.shape
    _, ps, dk = cache_k.shape
    page_spec = lambda p: pl.BlockSpec((None, ps, dk), lambda bi, pt: (pt[bi * n_pages + p], 0, 0))
    per_b = lambda r, c: pl.BlockSpec((None, r, c), lambda bi, pt: (bi, 0, 0))
    grid_spec = pltpu.PrefetchScalarGridSpec(
        num_scalar_prefetch=1, grid=(db,),
        in_specs=([pl.BlockSpec(memory_space=pltpu.SMEM), per_b(1, dk), per_b(1, dk), per_b(1, dk),
                   per_b(n_pages, ps), per_b(1, LANES)]
                  + [page_spec(p) for p in range(n_pages)] * 2),
        out_specs=per_b(1, dk),
        scratch_shapes=[pltpu.VMEM((n_pages * ps, LANES), F32), pltpu.VMEM((n_pages * ps, LANES), F32)])
    return pl.pallas_call(
        functools.partial(_dsa_sample_attend_kernel, n_pages=n_pages, ps=ps), grid_spec=grid_spec,
        out_shape=jax.ShapeDtypeStruct((db, 1, dk), F32),
        compiler_params=_cparams(("arbitrary",)), name="dsa_sample_attend",
    )(page_table.reshape(-1), rel_table, q, k_new, v_new, sel, sel_new,
      *([cache_k] * n_pages), *([cache_v] * n_pages))


def _mem_kv_kernel(m_ref, g_ref, w_ref, k_ref, v_ref, kbf_ref, vbf_ref):
    kv = _dot(_rms(m_ref[...], g_ref[...]).astype(BF16), w_ref[...])
    k, v = kv[:, :D_MEM_ATTN], kv[:, D_MEM_ATTN:]
    k_ref[...] = k
    v_ref[...] = v
    kbf_ref[...] = k.astype(BF16)
    vbf_ref[...] = v.astype(BF16)


def _mem_kv(mem, g, w):
    b, n, d = mem.shape
    blk = lambda c: pl.BlockSpec((None, n, c), lambda bi: (bi, 0, 0))
    const = lambda shape: pl.BlockSpec(shape, lambda bi: (0,) * len(shape))
    sds = lambda dt: jax.ShapeDtypeStruct((b, n, D_MEM_ATTN), dt)
    return pl.pallas_call(
        _mem_kv_kernel, grid=(b,), in_specs=[blk(d), const((1, d)), const(w.shape)],
        out_specs=(blk(D_MEM_ATTN),) * 4, out_shape=(sds(F32), sds(F32), sds(BF16), sds(BF16)),
        compiler_params=_cparams(("parallel",)), name="mem_kv",
    )(mem, g, w)


def _outproj_q(x, a, c, wo_ref, gq_ref, wcq_ref):
    x1 = x + _dot(a, wo_ref[0:D_ATTN, :]) + _dot(c, wo_ref[D_ATTN:, :])
    q = _dot(_rms(x1, gq_ref[...]).astype(BF16), wcq_ref[...]) * (MEM_HEAD_DIM ** -0.5)
    return x1, q.astype(BF16)


def _ffn_act(g, v):
    return (g * _sigmoid(g) * v).astype(BF16)


def _post_prompt_kernel(x_ref, a_ref, c_ref, mk_ref, mv_ref, wo_ref, gq_ref, wcq_ref, wco_ref, gf_ref,
                        wup_ref, fdw_ref, fdb_ref, wdn_ref, gfin_ref, y_ref, ffn_ref, ext_sc, *, tm, n_chunks):
    d_ff = wdn_ref.shape[0]
    hc = d_ff // n_chunks
    pad = SUBLANES

    @pl.when(pl.program_id(1) == 0)
    def _fresh_sequence():
        ext_sc[0:pad, :] = jnp.zeros((pad, ext_sc.shape[1]), F32)

    x1, qb = _outproj_q(x_ref[...], a_ref[...], c_ref[...], wo_ref, gq_ref, wcq_ref)
    head_of_lane = lax.broadcasted_iota(I32, (1, D_MEM_ATTN), 1) // MEM_HEAD_DIM
    mk = mk_ref[...]
    mv = mv_ref[...]
    o = jnp.zeros((tm, D_MEM_ATTN), F32)
    for h in range(MEM_HEADS):
        mine = head_of_lane == h
        lg = _dot_nt(jnp.where(mine, qb, jnp.zeros_like(qb)), mk)
        e = jnp.exp(lg - jnp.max(lg, axis=-1, keepdims=True))
        p = e * (1.0 / jnp.sum(e, axis=-1, keepdims=True))
        o = o + _dot(p.astype(BF16), jnp.where(mine, mv, jnp.zeros_like(mv)))
    x2 = x1 + _dot(o.astype(BF16), wco_ref[...])

    hn = _rms(x2, gf_ref[...]).astype(BF16)
    for cc in range(2 * n_chunks):
        ext_sc[pad:pad + tm, cc * hc:(cc + 1) * hc] = _dot(hn, wup_ref[:, cc * hc:(cc + 1) * hc])
    ffn_ref[...] = ext_sc[pad + tm - 2:pad + tm, :]

    def conv(lo):
        sl = slice(lo, lo + hc)
        out = fdb_ref[:, sl]
        for j in range(FFN_CONV_WIDTH):
            out = out + fdw_ref[j:j + 1, sl] * ext_sc[pad - 2 + j:pad - 2 + j + tm, sl]
        return out

    acc = jnp.zeros((tm, x_ref.shape[-1]), F32)
    for cc in range(n_chunks):
        acc = acc + _dot(_ffn_act(conv(cc * hc), conv(d_ff + cc * hc)), wdn_ref[cc * hc:(cc + 1) * hc, :])
    ext_sc[0:pad, :] = ext_sc[tm:tm + pad, :]
    y_ref[...] = _rms(x2 + acc, gfin_ref[...])


def _post_prompt(x, a, c, mk, mv, lw, gfin, tm):
    b, s, d = x.shape
    d_ff = lw['w_down'].shape[0]
    n_mem = mk.shape[1]
    tile = lambda width: pl.BlockSpec((None, tm, width), lambda bi, si: (bi, si, 0))
    const = lambda arr: pl.BlockSpec(arr.shape, lambda bi, si: (0,) * arr.ndim, pipeline_mode=pl.Buffered(1))
    per_b = lambda r, cdim: pl.BlockSpec((None, r, cdim), lambda bi, si: (bi, 0, 0))
    weights = [lw['w_o'], lw['norm_mem_q'], lw['w_cq'], lw['w_co'], lw['norm_ffn'], lw['w_up'],
               lw['ffn_dw'], lw['ffn_db'], lw['w_down'], gfin]
    return pl.pallas_call(
        functools.partial(_post_prompt_kernel, tm=tm, n_chunks=2), grid=(b, s // tm),
        in_specs=[tile(d), tile(D_ATTN), tile(a.shape[-1]), per_b(n_mem, D_MEM_ATTN), per_b(n_mem, D_MEM_ATTN)]
                 + [const(w) for w in weights],
        out_specs=(tile(d), per_b(FFN_CONV_WIDTH - 1, 2 * d_ff)),
        out_shape=(jax.ShapeDtypeStruct((b, s, d), F32), jax.ShapeDtypeStruct((b, FFN_CONV_WIDTH - 1, 2 * d_ff), F32)),
        scratch_shapes=[pltpu.VMEM((tm + SUBLANES, 2 * d_ff), F32)],
        compiler_params=_cparams(("arbitrary", "arbitrary")), name="post_prompt",
    )(x, a, c, mk, mv, *weights)


def _sample_outproj_kernel(x_ref, a_ref, c_ref, wo_ref, gq_ref, wcq_ref, x1_ref, q_ref):
    x1, qb = _outproj_q(x_ref[...], a_ref[...].astype(BF16), c_ref[...], wo_ref, gq_ref, wcq_ref)
    x1_ref[...] = x1
    q_ref[...] = qb.astype(F32)


def _sample_outproj(x, a, c, lw):
    return pl.pallas_call(
        _sample_outproj_kernel,
        out_shape=(jax.ShapeDtypeStruct(x.shape, F32), jax.ShapeDtypeStruct((x.shape[0], D_MEM_ATTN), F32)),
        compiler_params=pltpu.CompilerParams(vmem_limit_bytes=VMEM_LIMIT_BYTES), name="sample_outproj",
    )(x, a, c, lw['w_o'], lw['norm_mem_q'], lw['w_cq'])


def _sample_mem_attn_kernel(q_ref, mk_ref, mv_ref, o_ref, *, bb):
    seg = _head_select(D_MEM_ATTN, LANES, MEM_HEAD_DIM, by_row=True)
    expand = _head_select(LANES, D_MEM_ATTN, MEM_HEAD_DIM, by_row=False)

    def body(r, carry):
        q = q_ref[pl.ds(r, 1), :]
        hi, lo = _split_bf16(mk_ref[r].astype(BF16).astype(F32) * q)
        lg = _dot(hi, seg) + _dot(lo, seg)
        e = jnp.exp(lg - jnp.max(lg, axis=0, keepdims=True))
        p = e * (1.0 / jnp.sum(e, axis=0, keepdims=True))
        pe = _dot(p.astype(BF16), expand)
        o_ref[pl.ds(r, 1), :] = jnp.sum(pe * mv_ref[r], axis=0, keepdims=True)
        return carry

    lax.fori_loop(0, bb, body, 0)


def _sample_mem_attn(q, mk, mv, bb):
    db, n_mem, dm = mk.shape
    return pl.pallas_call(
        functools.partial(_sample_mem_attn_kernel, bb=bb), grid=(db // bb,),
        in_specs=[pl.BlockSpec((bb, dm), lambda i: (i, 0)),
                  pl.BlockSpec((bb, n_mem, dm), lambda i: (i, 0, 0)),
                  pl.BlockSpec((bb, n_mem, dm), lambda i: (i, 0, 0))],
        out_specs=pl.BlockSpec((bb, dm), lambda i: (i, 0)),
        out_shape=jax.ShapeDtypeStruct((db, dm), F32),
        compiler_params=_cparams(("parallel",)), name="sample_mem_attn",
    )(q, mk, mv)


def _sample_ffn_kernel(x1_ref, o_ref, h0_ref, h1_ref, wco_ref, gf_ref, wup_ref, fdw_ref, fdb_ref, wdn_ref,
                       gfin_ref, y_ref, up_ref, *, n_chunks):
    d_ff = wdn_ref.shape[0]
    hc = d_ff // n_chunks
    x2 = x1_ref[...] + _dot(o_ref[...].astype(BF16), wco_ref[...])
    hn = _rms(x2, gf_ref[...]).astype(BF16)

    def conv(lo):
        sl = slice(lo, lo + hc)
        up = _dot(hn, wup_ref[:, sl])
        up_ref[:, sl] = up
        return fdb_ref[:, sl] + fdw_ref[0:1, sl] * h0_ref[:, sl] + fdw_ref[1:2, sl] * h1_ref[:, sl] + fdw_ref[2:3, sl] * up

    acc = jnp.zeros(x1_ref.shape, F32)
    for cc in range(n_chunks):
        acc = acc + _dot(_ffn_act(conv(cc * hc), conv(d_ff + cc * hc)), wdn_ref[cc * hc:(cc + 1) * hc, :])
    y_ref[...] = _rms(x2 + acc, gfin_ref[...])


def _sample_ffn(x1, o, h0, h1, lw, gfin):
    d_ff = lw['w_down'].shape[0]
    return pl.pallas_call(
        functools.partial(_sample_ffn_kernel, n_chunks=2),
        out_shape=(jax.ShapeDtypeStruct(x1.shape, F32), jax.ShapeDtypeStruct((x1.shape[0], 2 * d_ff), F32)),
        compiler_params=pltpu.CompilerParams(vmem_limit_bytes=VMEM_LIMIT_BYTES), name="sample_ffn",
    )(x1, o, h0, h1, lw['w_co'], lw['norm_ffn'], lw['w_up'], lw['ffn_dw'], lw['ffn_db'], lw['w_down'], gfin)


def _prompt_layer(x, mem, rel_table, lw, gfin):
    b, s, _ = x.shape
    k, v, kiwi, q_bf, k_bf, vt_bf, qi_bf, ki_bf, u = _inproj(x, lw['norm_mix'], lw['w_cat'], tm=512)
    c_bf = _conv_prompt(u, lw['conv_dw'], lw['conv_db'], lw['conv_ln_g'], lw['conv_ln_b'], tm=512)
    a_bf = _dsa_prompt(rel_table, q_bf, qi_bf, kiwi, ki_bf, k_bf, vt_bf, tq=256)
    mk, mv, mk_bf, mv_bf = _mem_kv(mem, lw['norm_mem_kv'], lw['w_ckv'])
    y, ffn_state = _post_prompt(x, a_bf, c_bf, mk_bf, mv_bf, lw, gfin, tm=256)
    return y, k, v, kiwi[..., :IDX_DIM], u[:, s - (CONV_WIDTH - 1):], ffn_state, mk, mv


def _sample_layer(x, cache_k, cache_v, cache_idx_k, page_table, cache_conv, cache_ffn, mem_k, mem_v,
                  rel_table, lw, gfin):
    db = x.shape[0]
    n_pages = page_table.shape[1]
    ps = cache_k.shape[1]
    xs = x.reshape(1, db, -1)
    k, v, kiwi, q_bf, _, _, qi_bf, _, u = (t[0] for t in _inproj(xs, lw['norm_mix'], lw['w_cat'], tm=db))
    c_bf = _conv_sample(jnp.swapaxes(cache_conv, 0, 1), u, lw['conv_dw'], lw['conv_db'],
                        lw['conv_ln_g'], lw['conv_ln_b'])
    ki_new = kiwi[:, :IDX_DIM]
    w_idx = kiwi[:, IDX_DIM:IDX_DIM + N_IDX_HEADS]
    scores, snew = _dsa_sample_scores(
        page_table, qi_bf.astype(F32).reshape(db, N_IDX_HEADS, IDX_DIM), w_idx.reshape(db, N_IDX_HEADS, 1),
        ki_new.reshape(db, 1, IDX_DIM), cache_idx_k)
    sel, sel_new = _dsa_sample_select(scores.reshape(db, n_pages * ps), snew.reshape(db, LANES))
    a = _dsa_sample_attend(
        page_table, rel_table, q_bf.astype(F32).reshape(db, 1, D_ATTN), k.reshape(db, 1, D_ATTN),
        v.reshape(db, 1, D_ATTN), sel.reshape(db, n_pages, ps), sel_new.reshape(db, 1, LANES),
        cache_k.reshape(-1, ps, D_ATTN), cache_v.reshape(-1, ps, D_ATTN)).reshape(db, D_ATTN)
    x1, q_mem = _sample_outproj(x.reshape(db, -1), a, c_bf, lw)
    o = _sample_mem_attn(q_mem, mem_k.reshape(db, -1, D_MEM_ATTN), mem_v.reshape(db, -1, D_MEM_ATTN), bb=16)
    y, up = _sample_ffn(x1, o, cache_ffn[:, 0], cache_ffn[:, 1], lw, gfin)
    conv_state = jnp.concatenate([cache_conv[:, 1:], u[:, None, :]], axis=1)
    ffn_state = jnp.concatenate([cache_ffn[:, 1:], up[:, None, :]], axis=1)
    return y, k, v, ki_new, conv_state, ffn_state


def _layer_weights(l, norm_mix, w_in, conv_dw, conv_db, conv_ln_g, conv_ln_b, w_o, norm_mem_q, norm_mem_kv,
                   w_cq, w_ckv, w_co, norm_ffn, w_up, ffn_dw, ffn_db, w_down):
    n_main = 3 * D_ATTN + N_IDX_HEADS * IDX_DIM
    n_small = IDX_DIM + N_IDX_HEADS
    w = w_in[l]
    w_cat = jnp.concatenate(
        [w[:, :n_main], w[:, n_main + n_small:], w[:, n_main:n_main + n_small],
         jnp.zeros((w.shape[0], LANES - n_small), w.dtype)], axis=1).astype(BF16)
    row = lambda a: a[l].reshape(1, -1)
    return dict(
        norm_mix=row(norm_mix), w_cat=w_cat, conv_dw=conv_dw[l], conv_db=row(conv_db),
        conv_ln_g=row(conv_ln_g), conv_ln_b=row(conv_ln_b), w_o=w_o[l].astype(BF16),
        norm_mem_q=row(norm_mem_q), norm_mem_kv=row(norm_mem_kv), w_cq=w_cq[l].astype(BF16),
        w_ckv=w_ckv[l].astype(BF16), w_co=w_co[l].astype(BF16), norm_ffn=row(norm_ffn),
        w_up=w_up[l].astype(BF16), ffn_dw=ffn_dw[l], ffn_db=row(ffn_db), w_down=w_down[l].astype(BF16))


def kernel(x_prompt, x_sample, mem_prompt, cache_k, cache_v, cache_idx_k, page_table, cache_conv, cache_ffn,
           cache_mem_k, cache_mem_v, rel_bias, norm_mix, w_in, conv_dw, conv_db, conv_ln_g, conv_ln_b, w_o,
           norm_mem_q, norm_mem_kv, w_cq, w_ckv, w_co, norm_ffn, w_up, ffn_dw, ffn_db, w_down, norm_final):
    depth = w_in.shape[0]
    b, s, d = x_prompt.shape
    db = x_sample.shape[0]
    assert x_sample.shape[1] == 1
    yp, ys = x_prompt, x_sample
    outs = [[] for _ in range(12)]
    ones = jnp.ones((1, d), F32)
    for l in range(depth):
        lw = _layer_weights(l, norm_mix, w_in, conv_dw, conv_db, conv_ln_g, conv_ln_b, w_o, norm_mem_q,
                            norm_mem_kv, w_cq, w_ckv, w_co, norm_ffn, w_up, ffn_dw, ffn_db, w_down)
        assert depth == 1, "final-norm fusion below assumes a single layer"
        gfin = norm_final.reshape(1, d) if l == depth - 1 else ones
        yp, kp, vp, kip, cp, fp, mkp, mvp = _prompt_layer(yp, mem_prompt, rel_bias, lw, gfin)
        ys, k_s, v_s, ki_s, c_s, f_s = _sample_layer(
            ys, cache_k[l], cache_v[l], cache_idx_k[l], page_table, cache_conv[l], cache_ffn[l],
            cache_mem_k[l], cache_mem_v[l], rel_bias, lw, gfin)
        ys = ys.reshape(db, 1, d)
        vals = (kp.reshape(b, s, N_HEADS, HEAD_DIM), vp.reshape(b, s, N_HEADS, HEAD_DIM), kip,
                k_s.reshape(db, 1, N_HEADS, HEAD_DIM), v_s.reshape(db, 1, N_HEADS, HEAD_DIM),
                ki_s.reshape(db, 1, IDX_DIM), cp, c_s, fp, f_s,
                mkp.reshape(b, -1, MEM_HEADS, MEM_HEAD_DIM), mvp.reshape(b, -1, MEM_HEADS, MEM_HEAD_DIM))
        for lst, val in zip(outs, vals):
            lst.append(val)
    return (yp, ys) + tuple(jnp.stack(lst) for lst in outs)
```

```python
import functools
import math

import jax
import jax.numpy as jnp
from jax import lax
from jax.experimental import pallas as pl
from jax.experimental.pallas import tpu as pltpu

F32, BF16, I32 = jnp.float32, jnp.bfloat16, jnp.int32

N_HEADS = 8
HEAD_DIM = 64
D_ATTN = N_HEADS * HEAD_DIM
N_IDX_HEADS = 8
IDX_DIM = 64
TOPK_MAX = 256
CONV_WIDTH = 31
FFN_CONV_WIDTH = 3
MEM_HEADS = 4
MEM_HEAD_DIM = 64
D_MEM_ATTN = MEM_HEADS * MEM_HEAD_DIM
REL_BUCKETS = 32
REL_MAX_DIST = 128
RMS_EPS = 1e-6
LN_EPS = 1e-5
NEG_INF = -1e30
INT_MIN = -(2 ** 31)
LANES = 128
SUBLANES = 8
VMEM_LIMIT_BYTES = 56 * 1024 * 1024
CONV_HALO = 32


def _cparams(sem):
    return pltpu.CompilerParams(dimension_semantics=sem, vmem_limit_bytes=VMEM_LIMIT_BYTES)


def _dot(a, b):
    return jnp.dot(a, b, preferred_element_type=F32)


def _dot_nt(a, b):
    return lax.dot_general(a, b, (((1,), (1,)), ((), ())), preferred_element_type=F32)


def _rms(x, g):
    return x * lax.rsqrt(jnp.mean(x * x, -1, keepdims=True) + RMS_EPS) * g


def _sigmoid(x):
    return 1.0 / (1.0 + jnp.exp(-x))


def _sort_key(x):
    b = lax.bitcast_convert_type(x, I32)
    b = jnp.where(b == INT_MIN, 0, b)
    return b ^ ((b >> 31) & 0x7FFFFFFF)


def _rel_bucket(n):
    n = jnp.maximum(n, 0)
    max_exact = REL_BUCKETS // 2
    nf = jnp.maximum(n, 1).astype(F32)
    large = max_exact + (jnp.log(nf / max_exact) / math.log(REL_MAX_DIST / max_exact)
                         * (REL_BUCKETS - max_exact)).astype(I32)
    large = jnp.minimum(large, REL_BUCKETS - 1)
    return jnp.where(n < max_exact, n, large)


def _split_bf16(x):
    hi = x.astype(BF16)
    lo = (x - hi.astype(F32)).astype(BF16)
    return hi, lo


def _head_select(n_rows, n_cols, rows_per_head, by_row):
    r = lax.broadcasted_iota(I32, (n_rows, n_cols), 0)
    c = lax.broadcasted_iota(I32, (n_rows, n_cols), 1)
    hit = (r // rows_per_head == c) if by_row else (c // rows_per_head == r)
    return jnp.where(hit, 1.0, 0.0).astype(BF16)


def _inproj_kernel(x_ref, g_ref, w_ref, kt_ref, vt_ref, kit_ref, kiwi_ref, qbf_ref, kbf_ref, vtbf_ref,
                   qibf_ref, kibf_ref, u_ref):
    hb = _rms(x_ref[...], g_ref[...]).astype(BF16)

    def mm(lo, hi):
        return _dot(hb, w_ref[:, lo:hi])

    qbf_ref[...] = (mm(0, 512) * (HEAD_DIM ** -0.5)).astype(BF16)
    k = mm(512, 1024)
    kt_ref[...] = k.T
    kbf_ref[...] = k.astype(BF16)
    vt = mm(1024, 1536).T
    vt_ref[...] = vt
    vtbf_ref[...] = vt.astype(BF16)
    qibf_ref[...] = (mm(1536, 2048) * (IDX_DIM ** -0.5)).astype(BF16)
    u_ref[...] = mm(2048, 2560) * _sigmoid(mm(2560, 3072))
    kw = mm(3072, 3200)
    kiwi_ref[...] = kw
    kit_ref[...] = kw.T[:IDX_DIM, :]
    kibf_ref[...] = kw[:, :IDX_DIM].astype(BF16)


def _inproj(x, g, w_cat, tm):
    b, s, d = x.shape
    n_w = w_cat.shape[1]
    row = lambda width: pl.BlockSpec((None, tm, width), lambda bi, si: (bi, si, 0))
    const = lambda shape: pl.BlockSpec(shape, lambda bi, si: (0,) * len(shape))
    col = lambda height: pl.BlockSpec((None, height, tm), lambda bi, si: (bi, 0, si))
    out_shape = (
        jax.ShapeDtypeStruct((b, D_ATTN, s), F32),
        jax.ShapeDtypeStruct((b, D_ATTN, s), F32),
        jax.ShapeDtypeStruct((b, IDX_DIM, s), F32),
        jax.ShapeDtypeStruct((b, s, LANES), F32),
        jax.ShapeDtypeStruct((b, s, D_ATTN), BF16),
        jax.ShapeDtypeStruct((b, s, D_ATTN), BF16),
        jax.ShapeDtypeStruct((b, D_ATTN, s), BF16),
        jax.ShapeDtypeStruct((b, s, D_ATTN), BF16),
        jax.ShapeDtypeStruct((b, s, IDX_DIM), BF16),
        jax.ShapeDtypeStruct((b, s, D_ATTN), F32),
    )
    out_specs = (col(D_ATTN), col(D_ATTN), col(IDX_DIM), row(LANES), row(D_ATTN), row(D_ATTN),
                 col(D_ATTN), row(D_ATTN), row(IDX_DIM), row(D_ATTN))
    return pl.pallas_call(
        _inproj_kernel, grid=(b, s // tm),
        in_specs=[row(d), const((1, d)), const((d, n_w))],
        out_specs=out_specs, out_shape=out_shape,
        compiler_params=_cparams(("parallel", "parallel")), name="inproj",
    )(x, g, w_cat)


def _conv_ln_silu(acc, g, b):
    mu = jnp.mean(acc, -1, keepdims=True)
    xc = acc - mu
    var = jnp.mean(xc * xc, -1, keepdims=True)
    y = xc * lax.rsqrt(var + LN_EPS) * g + b
    return y * _sigmoid(y)


def _conv_prompt_kernel(u_ref, halo_ref, w_ref, b_ref, g_ref, bb_ref, c_ref, ext_sc, *, tm, rc):
    i = pl.program_id(1)
    ext_sc[0:CONV_HALO, :] = jnp.where(i == 0, 0.0, halo_ref[...])
    ext_sc[CONV_HALO:CONV_HALO + tm, :] = u_ref[...]
    first = CONV_HALO - (CONV_WIDTH - 1)
    for r in range(tm // rc):
        acc = jnp.zeros((rc, u_ref.shape[-1]), F32) + b_ref[...]
        for j in range(CONV_WIDTH):
            lo = r * rc + first + j
            acc = acc + w_ref[j:j + 1, :] * ext_sc[lo:lo + rc, :]
        c_ref[r * rc:(r + 1) * rc, :] = _conv_ln_silu(acc, g_ref[...], bb_ref[...]).astype(BF16)


def _conv_prompt(u, w, b, g, bb, tm):
    bsz, s, c = u.shape
    hb = tm // CONV_HALO
    const = lambda shape: pl.BlockSpec(shape, lambda bi, si: (0,) * len(shape))
    return pl.pallas_call(
        functools.partial(_conv_prompt_kernel, tm=tm, rc=64), grid=(bsz, s // tm),
        in_specs=[pl.BlockSpec((None, tm, c), lambda bi, si: (bi, si, 0)),
                  pl.BlockSpec((None, CONV_HALO, c), lambda bi, si: (bi, jnp.maximum(si * hb - 1, 0), 0)),
                  const((CONV_WIDTH, c)), const((1, c)), const((1, c)), const((1, c))],
        out_specs=pl.BlockSpec((None, tm, c), lambda bi, si: (bi, si, 0)),
        out_shape=jax.ShapeDtypeStruct((bsz, s, c), BF16),
        scratch_shapes=[pltpu.VMEM((tm + CONV_HALO, c), F32)],
        compiler_params=_cparams(("parallel", "parallel")), name="conv_prompt",
    )(u, u, w, b, g, bb)


def _conv_sample_kernel(hist_ref, u_ref, w_ref, b_ref, g_ref, bb_ref, c_ref):
    acc = w_ref[CONV_WIDTH - 1:CONV_WIDTH, :] * u_ref[...] + b_ref[...]
    for j in range(CONV_WIDTH - 1):
        acc = acc + w_ref[j:j + 1, :] * hist_ref[j]
    c_ref[...] = _conv_ln_silu(acc, g_ref[...], bb_ref[...]).astype(BF16)


def _conv_sample(hist_t, u, w, b, g, bb):
    return pl.pallas_call(
        _conv_sample_kernel, out_shape=jax.ShapeDtypeStruct(u.shape, BF16),
        compiler_params=pltpu.CompilerParams(vmem_limit_bytes=VMEM_LIMIT_BYTES), name="conv_sample",
    )(hist_t, u, w, b, g, bb)


def _dsa_prompt_kernel(tab_ref, q_ref, qi_ref, kiwi_ref, ki_ref, k_ref, vt_ref, a_ref,
                       keys_sc, qz_sc, qit_sc, wt_sc, bias_sc, madd_sc, s_sc, acc_sc, *, tq, topk, unroll):
    kc = tq
    b = pl.program_id(0)
    i = pl.program_id(1)

    @pl.when((b == 0) & (i == 0))
    def _build_bias():
        kk = lax.broadcasted_iota(I32, (kc, tq), 0)
        qq = lax.broadcasted_iota(I32, (kc, tq), 1)
        for d in range(3):
            n = d * tq + qq - kk
            bucket = _rel_bucket(n)
            for h in range(N_HEADS):
                val = jnp.zeros((kc, tq), F32)
                for bk in range(REL_BUCKETS):
                    val = jnp.where(bucket == bk, tab_ref[bk, h], val)
                if d == 0:
                    val = jnp.where(n >= 0, val, NEG_INF)
                bias_sc[d, h] = val
        s_sc[s_sc.shape[0] - kc:, :] = jnp.full((kc, tq), NEG_INF, F32)

    qt = q_ref[...].astype(F32).T
    zeros = jnp.zeros((HEAD_DIM, tq), BF16)
    for h in range(N_HEADS):
        blk = qt[h * HEAD_DIM:(h + 1) * HEAD_DIM, :].astype(BF16)
        qz_sc[h] = jnp.concatenate([blk, zeros] if h % 2 == 0 else [zeros, blk], axis=0)
    qit_sc[...] = qi_ref[...].astype(F32).T.astype(BF16)
    wt_sc[...] = kiwi_ref[...].T[IDX_DIM:IDX_DIM + N_IDX_HEADS, :] * (N_IDX_HEADS ** -0.5)

    nc = i + 1
    qpos = i * tq + lax.broadcasted_iota(I32, (kc, tq), 1)

    def chunk(c):
        return pl.ds(pl.multiple_of(c * kc, kc), kc)

    def score_body(c, carry):
        kchunk = ki_ref[chunk(c), :]
        acc = jnp.zeros((kc, tq), F32)
        for h in range(N_IDX_HEADS):
            s = _dot(kchunk, qit_sc[h * IDX_DIM:(h + 1) * IDX_DIM, :])
            acc = acc + wt_sc[h:h + 1, :] * jnp.maximum(s, 0.0)
        kpos = c * kc + lax.broadcasted_iota(I32, (kc, tq), 0)
        keys_sc[chunk(c), :] = jnp.where(kpos <= qpos, _sort_key(acc), INT_MIN)
        return carry

    lax.fori_loop(0, nc, score_body, 0)

    def count(pred):
        def body(c, cnt):
            ind = jnp.where(pred(keys_sc[chunk(c), :]), 1, 0).astype(I32)
            return cnt + ind.reshape(kc // SUBLANES, SUBLANES, tq).sum(axis=0)
        cnt8 = lax.fori_loop(0, nc, body, jnp.zeros((SUBLANES, tq), I32))
        return jnp.sum(cnt8, axis=0, keepdims=True)

    def bit_body(it, t):
        tc = t | jnp.left_shift(jnp.int32(1), 31 - it)
        tcs = tc ^ INT_MIN
        return jnp.where(count(lambda k: k >= tcs) >= topk, tc, t)

    thr = lax.fori_loop(0, 32, bit_body, jnp.zeros((1, tq), I32)) ^ INT_MIN

    tie = (count(lambda k: k >= thr) > topk) & (thr > INT_MIN)

    @pl.when(jnp.max(tie.astype(I32)) > 0)
    def _resolve_ties():
        need = (topk - count(lambda k: k > thr)).astype(F32)
        r = lax.broadcasted_iota(I32, (kc, kc), 0)
        cc = lax.broadcasted_iota(I32, (kc, kc), 1)
        lower = jnp.where(r >= cc, 1.0, 0.0).astype(BF16)

        def fix_body(c, seen):
            k = keys_sc[chunk(c), :]
            eq = k == thr
            eqf = jnp.where(eq, 1.0, 0.0)
            rank = _dot(lower, eqf.astype(BF16)) + seen
            drop = eq & (rank > need) & tie
            keys_sc[chunk(c), :] = jnp.where(drop, k - 1, k)
            return seen + jnp.sum(eqf, axis=0, keepdims=True)

        lax.fori_loop(0, nc, fix_body, jnp.zeros((1, tq), F32))

    def mask_body(c, carry):
        madd_sc[chunk(c), :] = jnp.where(keys_sc[chunk(c), :] >= thr, 0.0, NEG_INF)
        return carry

    lax.fori_loop(0, nc, mask_body, 0)

    def fold8(x, op):
        return op(x.reshape(kc // SUBLANES, SUBLANES, tq), axis=0)

    n_groups = (nc + unroll - 1) // unroll
    masked_chunk = s_sc.shape[0] // kc - 1
    for h in range(N_HEADS):
        hs = slice(h * HEAD_DIM, (h + 1) * HEAD_DIM)
        pair = (h // 2) * 2 * HEAD_DIM

        def logits_body(g, mx, h=h, pair=pair):
            for u in range(unroll):
                c = jnp.minimum(g * unroll + u, i)
                s = (_dot(k_ref[chunk(c), pair:pair + 2 * HEAD_DIM], qz_sc[h]) + madd_sc[chunk(c), :]
                     + bias_sc[jnp.minimum(i - c, 2), h])
                s_sc[chunk(c), :] = s
                mx = jnp.maximum(mx, fold8(s, jnp.max))
            return mx

        mx = lax.fori_loop(0, n_groups, logits_body, jnp.full((SUBLANES, tq), -3e38, F32))
        m = jnp.max(mx, axis=0, keepdims=True)

        def pv_body(g, carry, hs=hs, m=m):
            l8, acc = carry
            for u in range(unroll):
                c = g * unroll + u
                p = jnp.exp(s_sc[chunk(jnp.where(c <= i, c, masked_chunk)), :] - m)
                l8 = l8 + fold8(p, jnp.sum)
                acc = acc + _dot(vt_ref[hs, chunk(jnp.minimum(c, i))], p.astype(BF16))
            return l8, acc

        l8, acc = lax.fori_loop(0, n_groups, pv_body,
                                (jnp.zeros((SUBLANES, tq), F32), jnp.zeros((HEAD_DIM, tq), F32)))
        acc_sc[hs, :] = acc * (1.0 / jnp.sum(l8, axis=0, keepdims=True))
    a_ref[...] = acc_sc[...].T.astype(BF16)


def _dsa_prompt(rel_table, q_bf, qi_bf, kiwi, ki_bf, k_bf, vt_bf, tq):
    b, s, _ = q_bf.shape
    assert s % tq == 0 and tq >= REL_MAX_DIST
    topk = min(TOPK_MAX, s // 4)
    tile = lambda width: pl.BlockSpec((None, tq, width), lambda bi, si: (bi, si, 0))
    whole = lambda rows, cols: pl.BlockSpec((None, rows, cols), lambda bi, si: (bi, 0, 0))
    return pl.pallas_call(
        functools.partial(_dsa_prompt_kernel, tq=tq, topk=topk, unroll=4), grid=(b, s // tq),
        in_specs=[pl.BlockSpec(memory_space=pltpu.SMEM), tile(D_ATTN), tile(D_ATTN), tile(LANES),
                  whole(s, IDX_DIM), whole(s, D_ATTN), whole(D_ATTN, s)],
        out_specs=tile(D_ATTN), out_shape=jax.ShapeDtypeStruct((b, s, D_ATTN), BF16),
        scratch_shapes=[
            pltpu.VMEM((s, tq), I32),
            pltpu.VMEM((N_HEADS, 2 * HEAD_DIM, tq), BF16),
            pltpu.VMEM((D_ATTN, tq), BF16),
            pltpu.VMEM((N_IDX_HEADS, tq), F32),
            pltpu.VMEM((3, N_HEADS, tq, tq), F32),
            pltpu.VMEM((s, tq), F32),
            pltpu.VMEM((s + tq, tq), F32),
            pltpu.VMEM((D_ATTN, tq), F32),
        ],
        compiler_params=_cparams(("arbitrary", "arbitrary")), name="dsa_prompt",
    )(rel_table, q_bf, qi_bf, kiwi, ki_bf, k_bf, vt_bf)


def _dsa_sample_scores_kernel(pt_ref, qi_ref, w_ref, kinew_ref, *rest, n_pages, rows):
    pages, (sc_ref, snew_ref) = rest[:rows * n_pages], rest[rows * n_pages:]
    for r in range(rows):
        qh = qi_ref[r]
        w = w_ref[r] * (N_IDX_HEADS ** -0.5)
        qb = qh.astype(BF16)
        for p in range(n_pages):
            s = _dot(qb, pages[r * n_pages + p][...].astype(BF16))
            sc_ref[r, p:p + 1, :] = jnp.sum(w * jnp.maximum(s, 0.0), axis=0, keepdims=True)
        kn = kinew_ref[r].astype(BF16).astype(F32)
        sn = jnp.sum(qh * kn, axis=1, keepdims=True)
        tot = jnp.sum(w * jnp.maximum(sn, 0.0), axis=0, keepdims=True)
        snew_ref[r] = jnp.broadcast_to(tot, (1, LANES))


def _dsa_sample_scores(page_table, qi, w, ki_new, idx_pages_t, rows):
    db, n_pages = page_table.shape
    _, di, ps = idx_pages_t.shape
    assert db % rows == 0

    def page_spec(r, p):
        return pl.BlockSpec((None, di, ps), lambda bi, pt: (pt[(bi * rows + r) * n_pages + p], 0, 0))

    per_b = lambda r, c: pl.BlockSpec((rows, r, c), lambda bi, pt: (bi, 0, 0))
    grid_spec = pltpu.PrefetchScalarGridSpec(
        num_scalar_prefetch=1, grid=(db // rows,),
        in_specs=([per_b(N_IDX_HEADS, di), per_b(N_IDX_HEADS, 1), per_b(1, di)]
                  + [page_spec(r, p) for r in range(rows) for p in range(n_pages)]),
        out_specs=(per_b(n_pages, ps), per_b(1, LANES)))
    return pl.pallas_call(
        functools.partial(_dsa_sample_scores_kernel, n_pages=n_pages, rows=rows), grid_spec=grid_spec,
        out_shape=(jax.ShapeDtypeStruct((db, n_pages, ps), F32), jax.ShapeDtypeStruct((db, 1, LANES), F32)),
        compiler_params=_cparams(("arbitrary",)), name="dsa_sample_scores",
    )(page_table.reshape(-1), qi, w, ki_new, *([idx_pages_t] * (rows * n_pages)))


def _dsa_sample_select_kernel(sc_ref, snew_ref, sel_ref, selnew_ref, *, topk):
    keys = _sort_key(sc_ref[...])
    knew = _sort_key(snew_ref[...])[:, 0:1]
    rows, past = keys.shape

    def count(kmat, kn):
        return (jnp.sum(jnp.where(kmat, 1, 0).astype(I32), axis=1, keepdims=True)
                + jnp.where(kn, 1, 0).astype(I32))

    def bit_body(it, t):
        tc = t | jnp.left_shift(jnp.int32(1), 31 - it)
        tcs = tc ^ INT_MIN
        return jnp.where(count(keys >= tcs, knew >= tcs) >= topk, tc, t)

    thr = lax.fori_loop(0, 32, bit_body, jnp.zeros((rows, 1), I32)) ^ INT_MIN
    gt = keys > thr
    eq = keys == thr
    need = (topk - count(gt, knew > thr)).astype(F32)
    r = lax.broadcasted_iota(I32, (LANES, LANES), 0)
    c = lax.broadcasted_iota(I32, (LANES, LANES), 1)
    upper = jnp.where(r <= c, 1.0, 0.0).astype(BF16)
    seen = jnp.zeros((rows, 1), F32)
    for j in range(past // LANES):
        sl = slice(j * LANES, (j + 1) * LANES)
        eqf = jnp.where(eq[:, sl], 1.0, 0.0)
        rank = _dot(eqf.astype(BF16), upper) + seen
        sel_ref[:, sl] = jnp.where(gt[:, sl] | (eq[:, sl] & (rank <= need)), 1.0, 0.0)
        seen = seen + jnp.sum(eqf, axis=1, keepdims=True)
    keep_new = (knew > thr) | ((knew == thr) & (seen + 1.0 <= need))
    selnew_ref[...] = jnp.broadcast_to(jnp.where(keep_new, 1.0, 0.0), selnew_ref.shape)


def _dsa_sample_select(scores, snew):
    db, past = scores.shape
    topk = min(TOPK_MAX, (past + 1) // 4)
    return pl.pallas_call(
        functools.partial(_dsa_sample_select_kernel, topk=topk),
        out_shape=(jax.ShapeDtypeStruct((db, past), F32), jax.ShapeDtypeStruct((db, LANES), F32)),
        compiler_params=pltpu.CompilerParams(vmem_limit_bytes=VMEM_LIMIT_BYTES), name="dsa_sample_select",
    )(scores, snew)


def _dsa_sample_attend_kernel(pt_ref, tab_ref, q_ref, knew_ref, vnew_ref, sel_ref, selnew_ref, *rest,
                              n_pages, ps):
    kpages, vpages = rest[:n_pages], rest[n_pages:2 * n_pages]
    a_ref, bias_sc, lg_sc = rest[2 * n_pages:]
    past = n_pages * ps
    width = past + ps

    @pl.when(pl.program_id(0) == 0)
    def _build_bias():
        head = lax.broadcasted_iota(I32, (N_HEADS, 1), 0)
        pos = lax.broadcasted_iota(I32, (N_HEADS, width), 1)
        bucket = _rel_bucket(past - pos)
        val = jnp.zeros((N_HEADS, width), F32)
        for bk in range(REL_BUCKETS):
            col = jnp.zeros((N_HEADS, 1), F32)
            for h in range(N_HEADS):
                col = jnp.where(head == h, tab_ref[bk, h], col)
            val = jnp.where(bucket == bk, col, val)
        bias_sc[...] = val

    def as_columns(row):
        return jnp.broadcast_to(row, (ps, row.shape[1])).T

    qc = as_columns(q_ref[...])

    def head_logits(kt):
        prod = kt.astype(BF16).astype(F32) * qc
        return jnp.concatenate(
            [jnp.sum(prod[h * HEAD_DIM:(h + 1) * HEAD_DIM, :], axis=0, keepdims=True) for h in range(N_HEADS)], axis=0)

    for p in range(n_pages):
        lg_sc[:, p * ps:(p + 1) * ps] = head_logits(kpages[p][...])
    lg_sc[:, past:] = head_logits(as_columns(knew_ref[...]))
    lane = lax.broadcasted_iota(I32, (1, ps), 1)
    sel_all = jnp.concatenate([sel_ref[...], jnp.where(lane == 0, selnew_ref[...], 0.0)], axis=1)
    lg = jnp.where(sel_all > 0.0, lg_sc[...] + bias_sc[...], NEG_INF)
    e = jnp.exp(lg - jnp.max(lg, axis=1, keepdims=True))
    pb = (e * (1.0 / jnp.sum(e, axis=1, keepdims=True))).astype(BF16).astype(F32)

    acc = jnp.zeros((D_ATTN, ps), F32)
    for p in range(n_pages + 1):
        pe = jnp.concatenate(
            [jnp.broadcast_to(pb[h:h + 1, p * ps:(p + 1) * ps], (HEAD_DIM, ps)) for h in range(N_HEADS)], axis=0)
        acc = acc + pe * (vpages[p][...] if p < n_pages else as_columns(vnew_ref[...]))
    col = jnp.sum(acc, axis=1, keepdims=True)
    a_ref[...] = jnp.broadcast_to(col, (D_ATTN, ps)).T[0:1, :]


def _dsa_sample_attend(page_table, rel_table, q, k_new, v_new, sel, sel_new, k_pages_t, v_pages_t):
    db, n_pages = page_table.shape
    _, dk, ps = k_pages_t.shape
    assert ps == LANES and dk == D_ATTN
    page_spec = lambda p: pl.BlockSpec((None, dk, ps), lambda bi, pt: (pt[bi * n_pages + p], 0, 0))
    per_b = lambda r, c: pl.BlockSpec((None, r, c), lambda bi, pt: (bi, 0, 0))
    width = (n_pages + 1) * ps
    grid_spec = pltpu.PrefetchScalarGridSpec(
        num_scalar_prefetch=1, grid=(db,),
        in_specs=([pl.BlockSpec(memory_space=pltpu.SMEM), per_b(1, dk), per_b(1, dk), per_b(1, dk),
                   per_b(1, n_pages * ps), per_b(1, LANES)]
                  + [page_spec(p) for p in range(n_pages)] * 2),
        out_specs=per_b(1, dk),
        scratch_shapes=[pltpu.VMEM((N_HEADS, width), F32), pltpu.VMEM((N_HEADS, width), F32)])
    return pl.pallas_call(
        functools.partial(_dsa_sample_attend_kernel, n_pages=n_pages, ps=ps), grid_spec=grid_spec,
        out_shape=jax.ShapeDtypeStruct((db, 1, dk), F32),
        compiler_params=_cparams(("arbitrary",)), name="dsa_sample_attend",
    )(page_table.reshape(-1), rel_table, q, k_new, v_new, sel, sel_new,
      *([k_pages_t] * n_pages), *([v_pages_t] * n_pages))


def _mem_kv_kernel(m_ref, g_ref, w_ref, k_ref, v_ref, kbf_ref, vbf_ref):
    kv = _dot(_rms(m_ref[...], g_ref[...]).astype(BF16), w_ref[...])
    k, v = kv[:, :D_MEM_ATTN], kv[:, D_MEM_ATTN:]
    k_ref[...] = k
    v_ref[...] = v
    kbf_ref[...] = k.astype(BF16)
    vbf_ref[...] = v.astype(BF16)


def _mem_kv(mem, g, w):
    b, n, d = mem.shape
    blk = lambda c: pl.BlockSpec((None, n, c), lambda bi: (bi, 0, 0))
    const = lambda shape: pl.BlockSpec(shape, lambda bi: (0,) * len(shape))
    sds = lambda dt: jax.ShapeDtypeStruct((b, n, D_MEM_ATTN), dt)
    return pl.pallas_call(
        _mem_kv_kernel, grid=(b,), in_specs=[blk(d), const((1, d)), const(w.shape)],
        out_specs=(blk(D_MEM_ATTN),) * 4, out_shape=(sds(F32), sds(F32), sds(BF16), sds(BF16)),
        compiler_params=_cparams(("parallel",)), name="mem_kv",
    )(mem, g, w)


def _outproj_q(x, a, c, wo_ref, gq_ref, wcq_ref):
    x1 = x + _dot(a, wo_ref[0:D_ATTN, :]) + _dot(c, wo_ref[D_ATTN:, :])
    q = _dot(_rms(x1, gq_ref[...]).astype(BF16), wcq_ref[...]) * (MEM_HEAD_DIM ** -0.5)
    return x1, q.astype(BF16)


def _ffn_act(g, v):
    return (g * _sigmoid(g) * v).astype(BF16)


def _post_prompt_kernel(x_ref, a_ref, c_ref, mk_ref, mv_ref, wo_ref, gq_ref, wcq_ref, wco_ref, gf_ref,
                        wup_ref, fdw_ref, fdb_ref, wdn_ref, gfin_ref, y_ref, ffn_ref, ext_sc, *, tm, n_chunks):
    d_ff = wdn_ref.shape[0]
    hc = d_ff // n_chunks
    pad = SUBLANES

    @pl.when(pl.program_id(1) == 0)
    def _fresh_sequence():
        ext_sc[0:pad, :] = jnp.zeros((pad, ext_sc.shape[1]), F32)

    x1, qb = _outproj_q(x_ref[...], a_ref[...], c_ref[...], wo_ref, gq_ref, wcq_ref)
    head_of_lane = lax.broadcasted_iota(I32, (1, D_MEM_ATTN), 1) // MEM_HEAD_DIM
    mk = mk_ref[...]
    mv = mv_ref[...]
    o = jnp.zeros((tm, D_MEM_ATTN), F32)
    for h in range(MEM_HEADS):
        mine = head_of_lane == h
        lg = _dot_nt(jnp.where(mine, qb, jnp.zeros_like(qb)), mk)
        e = jnp.exp(lg - jnp.max(lg, axis=-1, keepdims=True))
        p = e * (1.0 / jnp.sum(e, axis=-1, keepdims=True))
        o = o + _dot(p.astype(BF16), jnp.where(mine, mv, jnp.zeros_like(mv)))
    x2 = x1 + _dot(o.astype(BF16), wco_ref[...])

    hn = _rms(x2, gf_ref[...]).astype(BF16)
    for cc in range(2 * n_chunks):
        ext_sc[pad:pad + tm, cc * hc:(cc + 1) * hc] = _dot(hn, wup_ref[:, cc * hc:(cc + 1) * hc])
    ffn_ref[...] = ext_sc[pad + tm - 2:pad + tm, :]

    def conv(lo):
        sl = slice(lo, lo + hc)
        out = fdb_ref[:, sl]
        for j in range(FFN_CONV_WIDTH):
            out = out + fdw_ref[j:j + 1, sl] * ext_sc[pad - 2 + j:pad - 2 + j + tm, sl]
        return out

    acc = jnp.zeros((tm, x_ref.shape[-1]), F32)
    for cc in range(n_chunks):
        acc = acc + _dot(_ffn_act(conv(cc * hc), conv(d_ff + cc * hc)), wdn_ref[cc * hc:(cc + 1) * hc, :])
    ext_sc[0:pad, :] = ext_sc[tm:tm + pad, :]
    y_ref[...] = _rms(x2 + acc, gfin_ref[...])


def _post_prompt(x, a, c, mk, mv, lw, gfin, tm):
    b, s, d = x.shape
    d_ff = lw['w_down'].shape[0]
    n_mem = mk.shape[1]
    tile = lambda width: pl.BlockSpec((None, tm, width), lambda bi, si: (bi, si, 0))
    const = lambda arr: pl.BlockSpec(arr.shape, lambda bi, si: (0,) * arr.ndim, pipeline_mode=pl.Buffered(1))
    per_b = lambda r, cdim: pl.BlockSpec((None, r, cdim), lambda bi, si: (bi, 0, 0))
    weights = [lw['w_o'], lw['norm_mem_q'], lw['w_cq'], lw['w_co'], lw['norm_ffn'], lw['w_up'],
               lw['ffn_dw'], lw['ffn_db'], lw['w_down'], gfin]
    return pl.pallas_call(
        functools.partial(_post_prompt_kernel, tm=tm, n_chunks=2), grid=(b, s // tm),
        in_specs=[tile(d), tile(D_ATTN), tile(a.shape[-1]), per_b(n_mem, D_MEM_ATTN), per_b(n_mem, D_MEM_ATTN)]
                 + [const(w) for w in weights],
        out_specs=(tile(d), per_b(FFN_CONV_WIDTH - 1, 2 * d_ff)),
        out_shape=(jax.ShapeDtypeStruct((b, s, d), F32), jax.ShapeDtypeStruct((b, FFN_CONV_WIDTH - 1, 2 * d_ff), F32)),
        scratch_shapes=[pltpu.VMEM((tm + SUBLANES, 2 * d_ff), F32)],
        compiler_params=_cparams(("arbitrary", "arbitrary")), name="post_prompt",
    )(x, a, c, mk, mv, *weights)


def _sample_outproj_kernel(x_ref, a_ref, c_ref, wo_ref, gq_ref, wcq_ref, x1_ref, q_ref):
    x1, qb = _outproj_q(x_ref[...], a_ref[...].astype(BF16), c_ref[...], wo_ref, gq_ref, wcq_ref)
    x1_ref[...] = x1
    q_ref[...] = qb.astype(F32)


def _sample_outproj(x, a, c, lw):
    return pl.pallas_call(
        _sample_outproj_kernel,
        out_shape=(jax.ShapeDtypeStruct(x.shape, F32), jax.ShapeDtypeStruct((x.shape[0], D_MEM_ATTN), F32)),
        compiler_params=pltpu.CompilerParams(vmem_limit_bytes=VMEM_LIMIT_BYTES), name="sample_outproj",
    )(x, a, c, lw['w_o'], lw['norm_mem_q'], lw['w_cq'])


def _sample_mem_attn_kernel(q_ref, mkt_ref, mvt_ref, o_ref, *, bb):
    dm, n_mem = mkt_ref.shape[1:]
    heads = range(MEM_HEADS)
    rows_of = lambda h: slice(h * MEM_HEAD_DIM, (h + 1) * MEM_HEAD_DIM)

    def body(r, carry):
        q = q_ref[pl.ds(r, 1), :]
        prod = mkt_ref[r].astype(BF16).astype(F32) * jnp.broadcast_to(q, (n_mem, dm)).T
        lg = jnp.concatenate([jnp.sum(prod[rows_of(h), :], axis=0, keepdims=True) for h in heads], axis=0)
        e = jnp.exp(lg - jnp.max(lg, axis=1, keepdims=True))
        pb = (e * (1.0 / jnp.sum(e, axis=1, keepdims=True))).astype(BF16).astype(F32)
        pe = jnp.concatenate([jnp.broadcast_to(pb[h:h + 1, :], (MEM_HEAD_DIM, n_mem)) for h in heads], axis=0)
        col = jnp.sum(pe * mvt_ref[r], axis=1, keepdims=True)
        o_ref[pl.ds(r, 1), :] = jnp.broadcast_to(col, (dm, LANES)).T[0:1, :]
        return carry

    lax.fori_loop(0, bb, body, 0)


def _sample_mem_attn(q, mk, mv, bb):
    db, dm, n_mem = mk.shape
    return pl.pallas_call(
        functools.partial(_sample_mem_attn_kernel, bb=bb), grid=(db // bb,),
        in_specs=[pl.BlockSpec((bb, dm), lambda i: (i, 0)),
                  pl.BlockSpec((bb, dm, n_mem), lambda i: (i, 0, 0)),
                  pl.BlockSpec((bb, dm, n_mem), lambda i: (i, 0, 0))],
        out_specs=pl.BlockSpec((bb, dm), lambda i: (i, 0)),
        out_shape=jax.ShapeDtypeStruct((db, dm), F32),
        compiler_params=_cparams(("parallel",)), name="sample_mem_attn",
    )(q, mk, mv)


def _sample_ffn_kernel(x1_ref, o_ref, h0_ref, h1_ref, wco_ref, gf_ref, wup_ref, fdw_ref, fdb_ref, wdn_ref,
                       gfin_ref, y_ref, up_ref, *, n_chunks):
    d_ff = wdn_ref.shape[0]
    hc = d_ff // n_chunks
    x2 = x1_ref[...] + _dot(o_ref[...].astype(BF16), wco_ref[...])
    hn = _rms(x2, gf_ref[...]).astype(BF16)

    def conv(lo):
        sl = slice(lo, lo + hc)
        up = _dot(hn, wup_ref[:, sl])
        up_ref[:, sl] = up
        return fdb_ref[:, sl] + fdw_ref[0:1, sl] * h0_ref[:, sl] + fdw_ref[1:2, sl] * h1_ref[:, sl] + fdw_ref[2:3, sl] * up

    acc = jnp.zeros(x1_ref.shape, F32)
    for cc in range(n_chunks):
        acc = acc + _dot(_ffn_act(conv(cc * hc), conv(d_ff + cc * hc)), wdn_ref[cc * hc:(cc + 1) * hc, :])
    y_ref[...] = _rms(x2 + acc, gfin_ref[...])


def _sample_ffn(x1, o, h0, h1, lw, gfin):
    d_ff = lw['w_down'].shape[0]
    return pl.pallas_call(
        functools.partial(_sample_ffn_kernel, n_chunks=2),
        out_shape=(jax.ShapeDtypeStruct(x1.shape, F32), jax.ShapeDtypeStruct((x1.shape[0], 2 * d_ff), F32)),
        compiler_params=pltpu.CompilerParams(vmem_limit_bytes=VMEM_LIMIT_BYTES), name="sample_ffn",
    )(x1, o, h0, h1, lw['w_co'], lw['norm_ffn'], lw['w_up'], lw['ffn_dw'], lw['ffn_db'], lw['w_down'], gfin)


def _prompt_layer(x, mem, rel_table, lw, gfin):
    b, s, _ = x.shape
    kt, vt, kit, kiwi, q_bf, k_bf, vt_bf, qi_bf, ki_bf, u = _inproj(x, lw['norm_mix'], lw['w_cat'], tm=512)
    c_bf = _conv_prompt(u, lw['conv_dw'], lw['conv_db'], lw['conv_ln_g'], lw['conv_ln_b'], tm=512)
    a_bf = _dsa_prompt(rel_table, q_bf, qi_bf, kiwi, ki_bf, k_bf, vt_bf, tq=256)
    mk, mv, mk_bf, mv_bf = _mem_kv(mem, lw['norm_mem_kv'], lw['w_ckv'])
    y, ffn_state = _post_prompt(x, a_bf, c_bf, mk_bf, mv_bf, lw, gfin, tm=256)
    heads_last = lambda t: jnp.transpose(t.reshape(b, N_HEADS, HEAD_DIM, s), (0, 3, 1, 2))
    return (y, heads_last(kt), heads_last(vt), jnp.swapaxes(kit, 1, 2), u[:, s - (CONV_WIDTH - 1):],
            ffn_state, mk, mv)


def _sample_layer(x, cache_k, cache_v, cache_idx_k, page_table, cache_conv, cache_ffn, mem_k, mem_v,
                  rel_table, lw, gfin):
    db = x.shape[0]
    n_pages = page_table.shape[1]
    ps = cache_k.shape[1]
    xs = x.reshape(1, db, -1)
    kt, vt, _, kiwi, q_bf, _, _, qi_bf, _, u = (t[0] for t in _inproj(xs, lw['norm_mix'], lw['w_cat'], tm=db))
    k, v = kt.T, vt.T
    c_bf = _conv_sample(jnp.swapaxes(cache_conv, 0, 1), u, lw['conv_dw'], lw['conv_db'],
                        lw['conv_ln_g'], lw['conv_ln_b'])
    ki_new = kiwi[:, :IDX_DIM]
    w_idx = kiwi[:, IDX_DIM:IDX_DIM + N_IDX_HEADS]
    pages_t = lambda c: jnp.moveaxis(c.reshape(c.shape[0], ps, -1), 1, 2)
    scores, snew = _dsa_sample_scores(
        page_table, qi_bf.astype(F32).reshape(db, N_IDX_HEADS, IDX_DIM), w_idx.reshape(db, N_IDX_HEADS, 1),
        ki_new.reshape(db, 1, IDX_DIM), pages_t(cache_idx_k), rows=4)
    sel, sel_new = _dsa_sample_select(scores.reshape(db, n_pages * ps), snew.reshape(db, LANES))
    a = _dsa_sample_attend(
        page_table, rel_table, q_bf.astype(F32).reshape(db, 1, D_ATTN), k.reshape(db, 1, D_ATTN),
        v.reshape(db, 1, D_ATTN), sel.reshape(db, 1, n_pages * ps), sel_new.reshape(db, 1, LANES),
        pages_t(cache_k), pages_t(cache_v)).reshape(db, D_ATTN)
    x1, q_mem = _sample_outproj(x.reshape(db, -1), a, c_bf, lw)
    mem_t = lambda m: jnp.moveaxis(m.reshape(db, -1, D_MEM_ATTN), 1, 2)
    o = _sample_mem_attn(q_mem, mem_t(mem_k), mem_t(mem_v), bb=16)
    y, up = _sample_ffn(x1, o, cache_ffn[:, 0], cache_ffn[:, 1], lw, gfin)
    conv_state = jnp.concatenate([cache_conv[:, 1:], u[:, None, :]], axis=1)
    ffn_state = jnp.concatenate([cache_ffn[:, 1:], up[:, None, :]], axis=1)
    return y, k, v, ki_new, conv_state, ffn_state


def _layer_weights(l, norm_mix, w_in, conv_dw, conv_db, conv_ln_g, conv_ln_b, w_o, norm_mem_q, norm_mem_kv,
                   w_cq, w_ckv, w_co, norm_ffn, w_up, ffn_dw, ffn_db, w_down):
    n_main = 3 * D_ATTN + N_IDX_HEADS * IDX_DIM
    n_small = IDX_DIM + N_IDX_HEADS
    w = w_in[l]
    w_cat = jnp.concatenate(
        [w[:, :n_main], w[:, n_main + n_small:], w[:, n_main:n_main + n_small],
         jnp.zeros((w.shape[0], LANES - n_small), w.dtype)], axis=1).astype(BF16)
    row = lambda a: a[l].reshape(1, -1)
    return dict(
        norm_mix=row(norm_mix), w_cat=w_cat, conv_dw=conv_dw[l], conv_db=row(conv_db),
        conv_ln_g=row(conv_ln_g), conv_ln_b=row(conv_ln_b), w_o=w_o[l].astype(BF16),
        norm_mem_q=row(norm_mem_q), norm_mem_kv=row(norm_mem_kv), w_cq=w_cq[l].astype(BF16),
        w_ckv=w_ckv[l].astype(BF16), w_co=w_co[l].astype(BF16), norm_ffn=row(norm_ffn),
        w_up=w_up[l].astype(BF16), ffn_dw=ffn_dw[l], ffn_db=row(ffn_db), w_down=w_down[l].astype(BF16))


def kernel(x_prompt, x_sample, mem_prompt, cache_k, cache_v, cache_idx_k, page_table, cache_conv, cache_ffn,
           cache_mem_k, cache_mem_v, rel_bias, norm_mix, w_in, conv_dw, conv_db, conv_ln_g, conv_ln_b, w_o,
           norm_mem_q, norm_mem_kv, w_cq, w_ckv, w_co, norm_ffn, w_up, ffn_dw, ffn_db, w_down, norm_final):
    depth = w_in.shape[0]
    b, s, d = x_prompt.shape
    db = x_sample.shape[0]
    assert x_sample.shape[1] == 1
    yp, ys = x_prompt, x_sample
    outs = [[] for _ in range(12)]
    ones = jnp.ones((1, d), F32)
    for l in range(depth):
        lw = _layer_weights(l, norm_mix, w_in, conv_dw, conv_db, conv_ln_g, conv_ln_b, w_o, norm_mem_q,
                            norm_mem_kv, w_cq, w_ckv, w_co, norm_ffn, w_up, ffn_dw, ffn_db, w_down)
        assert depth == 1, "final-norm fusion below assumes a single layer"
        gfin = norm_final.reshape(1, d) if l == depth - 1 else ones
        yp, kp, vp, kip, cp, fp, mkp, mvp = _prompt_layer(yp, mem_prompt, rel_bias, lw, gfin)
        ys, k_s, v_s, ki_s, c_s, f_s = _sample_layer(
            ys, cache_k[l], cache_v[l], cache_idx_k[l], page_table, cache_conv[l], cache_ffn[l],
            cache_mem_k[l], cache_mem_v[l], rel_bias, lw, gfin)
        ys = ys.reshape(db, 1, d)
        vals = (kp.reshape(b, s, N_HEADS, HEAD_DIM), vp.reshape(b, s, N_HEADS, HEAD_DIM), kip,
                k_s.reshape(db, 1, N_HEADS, HEAD_DIM), v_s.reshape(db, 1, N_HEADS, HEAD_DIM),
                ki_s.reshape(db, 1, IDX_DIM), cp, c_s, fp, f_s,
                mkp.reshape(b, -1, MEM_HEADS, MEM_HEAD_DIM), mvp.reshape(b, -1, MEM_HEADS, MEM_HEAD_DIM))
        for lst, val in zip(outs, vals):
            lst.append(val)
    return (yp, ys) + tuple(jnp.stack(lst) for lst in outs)
```

```python
import functools
import math

import jax
import jax.numpy as jnp
from jax import lax
from jax.experimental import pallas as pl
from jax.experimental.pallas import tpu as pltpu

F32, BF16, I32, I16 = jnp.float32, jnp.bfloat16, jnp.int32, jnp.int16

N_HEADS = 8
HEAD_DIM = 64
D_ATTN = N_HEADS * HEAD_DIM
N_IDX_HEADS = 8
IDX_DIM = 64
TOPK_MAX = 256
CONV_WIDTH = 31
FFN_CONV_WIDTH = 3
MEM_HEADS = 4
MEM_HEAD_DIM = 64
D_MEM_ATTN = MEM_HEADS * MEM_HEAD_DIM
REL_BUCKETS = 32
REL_MAX_DIST = 128
RMS_EPS = 1e-6
LN_EPS = 1e-5
NEG_INF = -1e30
INT_MIN = -(2 ** 31)
HALF_BIAS = 2 ** 15
LANES = 128
SUBLANES = 8
VMEM_LIMIT_BYTES = 56 * 1024 * 1024
CONV_HALO = 32


def _cparams(sem):
    return pltpu.CompilerParams(dimension_semantics=sem, vmem_limit_bytes=VMEM_LIMIT_BYTES)


def _dot(a, b):
    return jnp.dot(a, b, preferred_element_type=F32)


def _dot_nt(a, b):
    return lax.dot_general(a, b, (((1,), (1,)), ((), ())), preferred_element_type=F32)


def _rms(x, g):
    return x * lax.rsqrt(jnp.mean(x * x, -1, keepdims=True) + RMS_EPS) * g


def _sigmoid(x):
    return 1.0 / (1.0 + jnp.exp(-x))


def _sort_key(x):
    b = lax.bitcast_convert_type(x, I32)
    b = jnp.where(b == INT_MIN, 0, b)
    return b ^ ((b >> 31) & 0x7FFFFFFF)


def _rel_bucket(n):
    n = jnp.maximum(n, 0)
    max_exact = REL_BUCKETS // 2
    nf = jnp.maximum(n, 1).astype(F32)
    large = max_exact + (jnp.log(nf / max_exact) / math.log(REL_MAX_DIST / max_exact)
                         * (REL_BUCKETS - max_exact)).astype(I32)
    large = jnp.minimum(large, REL_BUCKETS - 1)
    return jnp.where(n < max_exact, n, large)


def _split_bf16(x):
    hi = x.astype(BF16)
    lo = (x - hi.astype(F32)).astype(BF16)
    return hi, lo


def _head_select(n_rows, n_cols, rows_per_head, by_row):
    r = lax.broadcasted_iota(I32, (n_rows, n_cols), 0)
    c = lax.broadcasted_iota(I32, (n_rows, n_cols), 1)
    hit = (r // rows_per_head == c) if by_row else (c // rows_per_head == r)
    return jnp.where(hit, 1.0, 0.0).astype(BF16)


def _inproj_kernel(x_ref, g_ref, w_ref, kt_ref, vt_ref, kit_ref, kiwi_ref, qbf_ref, kbf_ref, vtbf_ref,
                   qibf_ref, kibf_ref, u_ref):
    hb = _rms(x_ref[...], g_ref[...]).astype(BF16)

    def mm(lo, hi):
        return _dot(hb, w_ref[:, lo:hi])

    qbf_ref[...] = (mm(0, 512) * (HEAD_DIM ** -0.5)).astype(BF16)
    k = mm(512, 1024)
    kt_ref[...] = k.T
    kbf_ref[...] = k.astype(BF16)
    vt = mm(1024, 1536).T
    vt_ref[...] = vt
    vtbf_ref[...] = vt.astype(BF16)
    qibf_ref[...] = (mm(1536, 2048) * (IDX_DIM ** -0.5)).astype(BF16)
    u_ref[...] = mm(2048, 2560) * _sigmoid(mm(2560, 3072))
    kw = mm(3072, 3200)
    kiwi_ref[...] = kw
    kit_ref[...] = kw.T[:IDX_DIM, :]
    kibf_ref[...] = kw[:, :IDX_DIM].astype(BF16)


def _inproj(x, g, w_cat, tm):
    b, s, d = x.shape
    n_w = w_cat.shape[1]
    row = lambda width: pl.BlockSpec((None, tm, width), lambda bi, si: (bi, si, 0))
    const = lambda shape: pl.BlockSpec(shape, lambda bi, si: (0,) * len(shape))
    col = lambda height: pl.BlockSpec((None, height, tm), lambda bi, si: (bi, 0, si))
    out_shape = (
        jax.ShapeDtypeStruct((b, D_ATTN, s), F32),
        jax.ShapeDtypeStruct((b, D_ATTN, s), F32),
        jax.ShapeDtypeStruct((b, IDX_DIM, s), F32),
        jax.ShapeDtypeStruct((b, s, LANES), F32),
        jax.ShapeDtypeStruct((b, s, D_ATTN), BF16),
        jax.ShapeDtypeStruct((b, s, D_ATTN), BF16),
        jax.ShapeDtypeStruct((b, D_ATTN, s), BF16),
        jax.ShapeDtypeStruct((b, s, D_ATTN), BF16),
        jax.ShapeDtypeStruct((b, s, IDX_DIM), BF16),
        jax.ShapeDtypeStruct((b, s, D_ATTN), F32),
    )
    out_specs = (col(D_ATTN), col(D_ATTN), col(IDX_DIM), row(LANES), row(D_ATTN), row(D_ATTN),
                 col(D_ATTN), row(D_ATTN), row(IDX_DIM), row(D_ATTN))
    return pl.pallas_call(
        _inproj_kernel, grid=(b, s // tm),
        in_specs=[row(d), const((1, d)), const((d, n_w))],
        out_specs=out_specs, out_shape=out_shape,
        compiler_params=_cparams(("parallel", "parallel")), name="inproj",
    )(x, g, w_cat)


def _conv_ln_silu(acc, g, b):
    mu = jnp.mean(acc, -1, keepdims=True)
    xc = acc - mu
    var = jnp.mean(xc * xc, -1, keepdims=True)
    y = xc * lax.rsqrt(var + LN_EPS) * g + b
    return y * _sigmoid(y)


def _conv_prompt_kernel(u_ref, halo_ref, w_ref, b_ref, g_ref, bb_ref, c_ref, ext_sc, *, tm, rc):
    i = pl.program_id(1)
    ext_sc[0:CONV_HALO, :] = jnp.where(i == 0, 0.0, halo_ref[...])
    ext_sc[CONV_HALO:CONV_HALO + tm, :] = u_ref[...]
    first = CONV_HALO - (CONV_WIDTH - 1)
    for r in range(tm // rc):
        acc = jnp.zeros((rc, u_ref.shape[-1]), F32) + b_ref[...]
        for j in range(CONV_WIDTH):
            lo = r * rc + first + j
            acc = acc + w_ref[j:j + 1, :] * ext_sc[lo:lo + rc, :]
        c_ref[r * rc:(r + 1) * rc, :] = _conv_ln_silu(acc, g_ref[...], bb_ref[...]).astype(BF16)


def _conv_prompt(u, w, b, g, bb, tm):
    bsz, s, c = u.shape
    hb = tm // CONV_HALO
    const = lambda shape: pl.BlockSpec(shape, lambda bi, si: (0,) * len(shape))
    return pl.pallas_call(
        functools.partial(_conv_prompt_kernel, tm=tm, rc=64), grid=(bsz, s // tm),
        in_specs=[pl.BlockSpec((None, tm, c), lambda bi, si: (bi, si, 0)),
                  pl.BlockSpec((None, CONV_HALO, c), lambda bi, si: (bi, jnp.maximum(si * hb - 1, 0), 0)),
                  const((CONV_WIDTH, c)), const((1, c)), const((1, c)), const((1, c))],
        out_specs=pl.BlockSpec((None, tm, c), lambda bi, si: (bi, si, 0)),
        out_shape=jax.ShapeDtypeStruct((bsz, s, c), BF16),
        scratch_shapes=[pltpu.VMEM((tm + CONV_HALO, c), F32)],
        compiler_params=_cparams(("parallel", "parallel")), name="conv_prompt",
    )(u, u, w, b, g, bb)


def _conv_sample_kernel(hist_ref, u_ref, w_ref, b_ref, g_ref, bb_ref, c_ref):
    acc = w_ref[CONV_WIDTH - 1:CONV_WIDTH, :] * u_ref[...] + b_ref[...]
    for j in range(CONV_WIDTH - 1):
        acc = acc + w_ref[j:j + 1, :] * hist_ref[j]
    c_ref[...] = _conv_ln_silu(acc, g_ref[...], bb_ref[...]).astype(BF16)


def _conv_sample(hist_t, u, w, b, g, bb):
    return pl.pallas_call(
        _conv_sample_kernel, out_shape=jax.ShapeDtypeStruct(u.shape, BF16),
        compiler_params=pltpu.CompilerParams(vmem_limit_bytes=VMEM_LIMIT_BYTES), name="conv_sample",
    )(hist_t, u, w, b, g, bb)


def _dsa_prompt_kernel(tab_ref, q_ref, qi_ref, kiwi_ref, ki_ref, k_ref, vt_ref, a_ref,
                       keys_sc, hi_sc, lo_sc, qz_sc, qit_sc, wt_sc, bias_sc, madd_sc, s0_sc, s1_sc, acc_sc,
                       *, tq, topk, unroll):
    kc = tq
    b = pl.program_id(0)
    i = pl.program_id(1)

    @pl.when((b == 0) & (i == 0))
    def _build_bias():
        kk = lax.broadcasted_iota(I32, (kc, tq), 0)
        qq = lax.broadcasted_iota(I32, (kc, tq), 1)
        for d in range(3):
            n = d * tq + qq - kk
            bucket = _rel_bucket(n)
            for h in range(N_HEADS):
                val = jnp.zeros((kc, tq), F32)
                for bk in range(REL_BUCKETS):
                    val = jnp.where(bucket == bk, tab_ref[bk, h], val)
                if d == 0:
                    val = jnp.where(n >= 0, val, NEG_INF)
                bias_sc[d, h] = val
        for buf in (s0_sc, s1_sc):
            buf[buf.shape[0] - kc:, :] = jnp.full((kc, tq), NEG_INF, F32)

    qt = q_ref[...].astype(F32).T
    zeros = jnp.zeros((HEAD_DIM, tq), BF16)
    for h in range(N_HEADS):
        blk = qt[h * HEAD_DIM:(h + 1) * HEAD_DIM, :].astype(BF16)
        qz_sc[h] = jnp.concatenate([blk, zeros] if h % 2 == 0 else [zeros, blk], axis=0)
    qit_sc[...] = qi_ref[...].astype(F32).T.astype(BF16)
    wt_sc[...] = kiwi_ref[...].T[IDX_DIM:IDX_DIM + N_IDX_HEADS, :] * (N_IDX_HEADS ** -0.5)

    nc = i + 1
    qpos = i * tq + lax.broadcasted_iota(I32, (kc, tq), 1)

    def chunk(c):
        return pl.ds(pl.multiple_of(c * kc, kc), kc)

    def score_body(c, carry):
        kchunk = ki_ref[chunk(c), :]
        acc = jnp.zeros((kc, tq), F32)
        for h in range(N_IDX_HEADS):
            s = _dot(kchunk, qit_sc[h * IDX_DIM:(h + 1) * IDX_DIM, :])
            acc = acc + wt_sc[h:h + 1, :] * jnp.maximum(s, 0.0)
        kpos = c * kc + lax.broadcasted_iota(I32, (kc, tq), 0)
        key = jnp.where(kpos <= qpos, _sort_key(acc), INT_MIN)
        keys_sc[chunk(c), :] = key
        hi_sc[chunk(c), :] = (key >> 16).astype(I16)
        lo_sc[chunk(c), :] = ((key & 0xFFFF) - HALF_BIAS).astype(I16)
        return carry

    lax.fori_loop(0, nc, score_body, 0)

    def count(pred):
        def body(c, cnt):
            ind = jnp.where(pred(keys_sc[chunk(c), :]), 1, 0).astype(I32)
            return cnt + ind.reshape(kc // SUBLANES, SUBLANES, tq).sum(axis=0)
        cnt8 = lax.fori_loop(0, nc, body, jnp.zeros((SUBLANES, tq), I32))
        return jnp.sum(cnt8, axis=0, keepdims=True)

    def count16(ref, pred):
        rows = 2 * SUBLANES

        def body(c, cnt):
            ind = jnp.where(pred(ref[chunk(c), :]), jnp.int16(1), jnp.int16(0))
            for j in range(kc // rows):
                cnt = cnt + ind[j * rows:(j + 1) * rows, :]
            return cnt
        cnt16 = lax.fori_loop(0, nc, body, jnp.zeros((rows, tq), I16))
        return jnp.sum(cnt16.astype(I32), axis=0, keepdims=True)

    def kth_largest16(ref, kth):
        def bit_body(it, t):
            tc = t | jnp.left_shift(jnp.int32(1), 15 - it)
            tcs = (tc - HALF_BIAS).astype(I16)
            return jnp.where(count16(ref, lambda x: x >= tcs) >= kth, tc, t)
        return lax.fori_loop(0, 16, bit_body, jnp.zeros((1, tq), I32))

    t_hi = kth_largest16(hi_sc, topk) - HALF_BIAS
    t_hi16 = t_hi.astype(I16)
    need_lo = topk - count16(hi_sc, lambda x: x > t_hi16)

    def low_body(c, carry):
        lo_sc[chunk(c), :] = jnp.where(hi_sc[chunk(c), :] == t_hi16, lo_sc[chunk(c), :], jnp.int16(-HALF_BIAS))
        return carry

    lax.fori_loop(0, nc, low_body, 0)
    thr = t_hi * (2 * HALF_BIAS) + kth_largest16(lo_sc, need_lo)

    tie = (count(lambda k: k >= thr) > topk) & (thr > INT_MIN)

    @pl.when(jnp.max(tie.astype(I32)) > 0)
    def _resolve_ties():
        need = (topk - count(lambda k: k > thr)).astype(F32)
        r = lax.broadcasted_iota(I32, (kc, kc), 0)
        cc = lax.broadcasted_iota(I32, (kc, kc), 1)
        lower = jnp.where(r >= cc, 1.0, 0.0).astype(BF16)

        def fix_body(c, seen):
            k = keys_sc[chunk(c), :]
            eq = k == thr
            eqf = jnp.where(eq, 1.0, 0.0)
            rank = _dot(lower, eqf.astype(BF16)) + seen
            drop = eq & (rank > need) & tie
            keys_sc[chunk(c), :] = jnp.where(drop, k - 1, k)
            return seen + jnp.sum(eqf, axis=0, keepdims=True)

        lax.fori_loop(0, nc, fix_body, jnp.zeros((1, tq), F32))

    def mask_body(c, carry):
        madd_sc[chunk(c), :] = jnp.where(keys_sc[chunk(c), :] >= thr, 0.0, NEG_INF)
        return carry

    lax.fori_loop(0, nc, mask_body, 0)

    def fold8(x, op):
        return op(x.reshape(kc // SUBLANES, SUBLANES, tq), axis=0)

    n_groups = (nc + unroll - 1) // unroll
    masked_chunk = s0_sc.shape[0] // kc - 1
    s_bufs = (s0_sc, s1_sc)

    def logits_step(h, c, mx):
        pair = (h // 2) * 2 * HEAD_DIM
        s = (_dot(k_ref[chunk(c), pair:pair + 2 * HEAD_DIM], qz_sc[h]) + madd_sc[chunk(c), :]
             + bias_sc[jnp.minimum(i - c, 2), h])
        s_bufs[h % 2][chunk(c), :] = s
        return jnp.maximum(mx, fold8(s, jnp.max))

    def pv_step(h, c, m, l8, acc):
        p = jnp.exp(s_bufs[h % 2][chunk(jnp.where(c <= i, c, masked_chunk)), :] - m)
        hs = slice(h * HEAD_DIM, (h + 1) * HEAD_DIM)
        return l8 + fold8(p, jnp.sum), acc + _dot(vt_ref[hs, chunk(jnp.minimum(c, i))], p.astype(BF16))

    def head_loop(h_logits, h_pv, m):
        def body(g, carry):
            mx, l8, acc = carry
            for u in range(unroll):
                c = g * unroll + u
                if h_logits is not None:
                    mx = logits_step(h_logits, jnp.minimum(c, i), mx)
                if h_pv is not None:
                    l8, acc = pv_step(h_pv, c, m, l8, acc)
            return mx, l8, acc
        return lax.fori_loop(0, n_groups, body, (jnp.full((SUBLANES, tq), -3e38, F32),
                                                 jnp.zeros((SUBLANES, tq), F32), jnp.zeros((HEAD_DIM, tq), F32)))

    m = None
    for h in range(N_HEADS + 1):
        mx, l8, acc = head_loop(h if h < N_HEADS else None, h - 1 if h > 0 else None, m)
        if h > 0:
            hs = slice((h - 1) * HEAD_DIM, h * HEAD_DIM)
            acc_sc[hs, :] = acc * (1.0 / jnp.sum(l8, axis=0, keepdims=True))
        m = jnp.max(mx, axis=0, keepdims=True)
    a_ref[...] = acc_sc[...].T.astype(BF16)


def _dsa_prompt(rel_table, q_bf, qi_bf, kiwi, ki_bf, k_bf, vt_bf, tq):
    b, s, _ = q_bf.shape
    assert s % tq == 0 and tq >= REL_MAX_DIST
    topk = min(TOPK_MAX, s // 4)
    tile = lambda width: pl.BlockSpec((None, tq, width), lambda bi, si: (bi, si, 0))
    whole = lambda rows, cols: pl.BlockSpec((None, rows, cols), lambda bi, si: (bi, 0, 0))
    return pl.pallas_call(
        functools.partial(_dsa_prompt_kernel, tq=tq, topk=topk, unroll=4), grid=(b, s // tq),
        in_specs=[pl.BlockSpec(memory_space=pltpu.SMEM), tile(D_ATTN), tile(D_ATTN), tile(LANES),
                  whole(s, IDX_DIM), whole(s, D_ATTN), whole(D_ATTN, s)],
        out_specs=tile(D_ATTN), out_shape=jax.ShapeDtypeStruct((b, s, D_ATTN), BF16),
        scratch_shapes=[
            pltpu.VMEM((s, tq), I32),
            pltpu.VMEM((s, tq), I16), pltpu.VMEM((s, tq), I16),
            pltpu.VMEM((N_HEADS, 2 * HEAD_DIM, tq), BF16),
            pltpu.VMEM((D_ATTN, tq), BF16),
            pltpu.VMEM((N_IDX_HEADS, tq), F32),
            pltpu.VMEM((3, N_HEADS, tq, tq), F32),
            pltpu.VMEM((s, tq), F32),
            pltpu.VMEM((s + tq, tq), F32), pltpu.VMEM((s + tq, tq), F32),
            pltpu.VMEM((D_ATTN, tq), F32),
        ],
        compiler_params=_cparams(("arbitrary", "arbitrary")), name="dsa_prompt",
    )(rel_table, q_bf, qi_bf, kiwi, ki_bf, k_bf, vt_bf)


def _dsa_sample_scores_kernel(pt_ref, qi_ref, w_ref, kinew_ref, *rest, n_pages, rows):
    pages, (sc_ref, snew_ref) = rest[:rows * n_pages], rest[rows * n_pages:]
    for r in range(rows):
        qh = qi_ref[r]
        w = w_ref[r] * (N_IDX_HEADS ** -0.5)
        qb = qh.astype(BF16)
        for p in range(n_pages):
            s = _dot(qb, pages[r * n_pages + p][...].astype(BF16))
            sc_ref[r, p:p + 1, :] = jnp.sum(w * jnp.maximum(s, 0.0), axis=0, keepdims=True)
        kn = kinew_ref[r].astype(BF16).astype(F32)
        sn = jnp.sum(qh * kn, axis=1, keepdims=True)
        tot = jnp.sum(w * jnp.maximum(sn, 0.0), axis=0, keepdims=True)
        snew_ref[r] = jnp.broadcast_to(tot, (1, LANES))


def _dsa_sample_scores(page_table, qi, w, ki_new, idx_pages_t, rows):
    db, n_pages = page_table.shape
    _, di, ps = idx_pages_t.shape
    assert db % rows == 0

    def page_spec(r, p):
        return pl.BlockSpec((None, di, ps), lambda bi, pt: (pt[(bi * rows + r) * n_pages + p], 0, 0))

    per_b = lambda r, c: pl.BlockSpec((rows, r, c), lambda bi, pt: (bi, 0, 0))
    grid_spec = pltpu.PrefetchScalarGridSpec(
        num_scalar_prefetch=1, grid=(db // rows,),
        in_specs=([per_b(N_IDX_HEADS, di), per_b(N_IDX_HEADS, 1), per_b(1, di)]
                  + [page_spec(r, p) for r in range(rows) for p in range(n_pages)]),
        out_specs=(per_b(n_pages, ps), per_b(1, LANES)))
    return pl.pallas_call(
        functools.partial(_dsa_sample_scores_kernel, n_pages=n_pages, rows=rows), grid_spec=grid_spec,
        out_shape=(jax.ShapeDtypeStruct((db, n_pages, ps), F32), jax.ShapeDtypeStruct((db, 1, LANES), F32)),
        compiler_params=_cparams(("arbitrary",)), name="dsa_sample_scores",
    )(page_table.reshape(-1), qi, w, ki_new, *([idx_pages_t] * (rows * n_pages)))


def _dsa_sample_select_kernel(sc_ref, snew_ref, sel_ref, selnew_ref, *, topk):
    keys = _sort_key(sc_ref[...])
    knew = _sort_key(snew_ref[...])[:, 0:1]
    rows, past = keys.shape

    def count(kmat, kn):
        return (jnp.sum(jnp.where(kmat, 1, 0).astype(I32), axis=1, keepdims=True)
                + jnp.where(kn, 1, 0).astype(I32))

    def bit_body(it, t):
        tc = t | jnp.left_shift(jnp.int32(1), 31 - it)
        tcs = tc ^ INT_MIN
        return jnp.where(count(keys >= tcs, knew >= tcs) >= topk, tc, t)

    thr = lax.fori_loop(0, 32, bit_body, jnp.zeros((rows, 1), I32)) ^ INT_MIN
    gt = keys > thr
    eq = keys == thr
    need = (topk - count(gt, knew > thr)).astype(F32)
    r = lax.broadcasted_iota(I32, (LANES, LANES), 0)
    c = lax.broadcasted_iota(I32, (LANES, LANES), 1)
    upper = jnp.where(r <= c, 1.0, 0.0).astype(BF16)
    seen = jnp.zeros((rows, 1), F32)
    for j in range(past // LANES):
        sl = slice(j * LANES, (j + 1) * LANES)
        eqf = jnp.where(eq[:, sl], 1.0, 0.0)
        rank = _dot(eqf.astype(BF16), upper) + seen
        sel_ref[:, sl] = jnp.where(gt[:, sl] | (eq[:, sl] & (rank <= need)), 1.0, 0.0)
        seen = seen + jnp.sum(eqf, axis=1, keepdims=True)
    keep_new = (knew > thr) | ((knew == thr) & (seen + 1.0 <= need))
    selnew_ref[...] = jnp.broadcast_to(jnp.where(keep_new, 1.0, 0.0), selnew_ref.shape)


def _dsa_sample_select(scores, snew):
    db, past = scores.shape
    topk = min(TOPK_MAX, (past + 1) // 4)
    return pl.pallas_call(
        functools.partial(_dsa_sample_select_kernel, topk=topk),
        out_shape=(jax.ShapeDtypeStruct((db, past), F32), jax.ShapeDtypeStruct((db, LANES), F32)),
        compiler_params=pltpu.CompilerParams(vmem_limit_bytes=VMEM_LIMIT_BYTES), name="dsa_sample_select",
    )(scores, snew)


def _dsa_sample_attend_kernel(pt_ref, tab_ref, q_ref, knew_ref, vnew_ref, sel_ref, selnew_ref, *rest,
                              n_pages, ps):
    kpages, vpages = rest[:n_pages], rest[n_pages:2 * n_pages]
    a_ref, bias_sc, lg_sc = rest[2 * n_pages:]
    past = n_pages * ps
    width = past + ps

    @pl.when(pl.program_id(0) == 0)
    def _build_bias():
        head = lax.broadcasted_iota(I32, (N_HEADS, 1), 0)
        pos = lax.broadcasted_iota(I32, (N_HEADS, width), 1)
        bucket = _rel_bucket(past - pos)
        val = jnp.zeros((N_HEADS, width), F32)
        for bk in range(REL_BUCKETS):
            col = jnp.zeros((N_HEADS, 1), F32)
            for h in range(N_HEADS):
                col = jnp.where(head == h, tab_ref[bk, h], col)
            val = jnp.where(bucket == bk, col, val)
        bias_sc[...] = val

    def as_columns(row):
        return jnp.broadcast_to(row, (ps, row.shape[1])).T

    qc = as_columns(q_ref[...])

    def head_logits(kt):
        prod = kt.astype(BF16).astype(F32) * qc
        return jnp.concatenate(
            [jnp.sum(prod[h * HEAD_DIM:(h + 1) * HEAD_DIM, :], axis=0, keepdims=True) for h in range(N_HEADS)], axis=0)

    for p in range(n_pages):
        lg_sc[:, p * ps:(p + 1) * ps] = head_logits(kpages[p][...])
    lg_sc[:, past:] = head_logits(as_columns(knew_ref[...]))
    lane = lax.broadcasted_iota(I32, (1, ps), 1)
    sel_all = jnp.concatenate([sel_ref[...], jnp.where(lane == 0, selnew_ref[...], 0.0)], axis=1)
    lg = jnp.where(sel_all > 0.0, lg_sc[...] + bias_sc[...], NEG_INF)
    e = jnp.exp(lg - jnp.max(lg, axis=1, keepdims=True))
    pb = (e * (1.0 / jnp.sum(e, axis=1, keepdims=True))).astype(BF16).astype(F32)

    acc = jnp.zeros((D_ATTN, ps), F32)
    for p in range(n_pages + 1):
        pe = jnp.concatenate(
            [jnp.broadcast_to(pb[h:h + 1, p * ps:(p + 1) * ps], (HEAD_DIM, ps)) for h in range(N_HEADS)], axis=0)
        acc = acc + pe * (vpages[p][...] if p < n_pages else as_columns(vnew_ref[...]))
    col = jnp.sum(acc, axis=1, keepdims=True)
    a_ref[...] = jnp.broadcast_to(col, (D_ATTN, ps)).T[0:1, :]


def _dsa_sample_attend(page_table, rel_table, q, k_new, v_new, sel, sel_new, k_pages_t, v_pages_t):
    db, n_pages = page_table.shape
    _, dk, ps = k_pages_t.shape
    assert ps == LANES and dk == D_ATTN
    page_spec = lambda p: pl.BlockSpec((None, dk, ps), lambda bi, pt: (pt[bi * n_pages + p], 0, 0))
    per_b = lambda r, c: pl.BlockSpec((None, r, c), lambda bi, pt: (bi, 0, 0))
    width = (n_pages + 1) * ps
    grid_spec = pltpu.PrefetchScalarGridSpec(
        num_scalar_prefetch=1, grid=(db,),
        in_specs=([pl.BlockSpec(memory_space=pltpu.SMEM), per_b(1, dk), per_b(1, dk), per_b(1, dk),
                   per_b(1, n_pages * ps), per_b(1, LANES)]
                  + [page_spec(p) for p in range(n_pages)] * 2),
        out_specs=per_b(1, dk),
        scratch_shapes=[pltpu.VMEM((N_HEADS, width), F32), pltpu.VMEM((N_HEADS, width), F32)])
    return pl.pallas_call(
        functools.partial(_dsa_sample_attend_kernel, n_pages=n_pages, ps=ps), grid_spec=grid_spec,
        out_shape=jax.ShapeDtypeStruct((db, 1, dk), F32),
        compiler_params=_cparams(("arbitrary",)), name="dsa_sample_attend",
    )(page_table.reshape(-1), rel_table, q, k_new, v_new, sel, sel_new,
      *([k_pages_t] * n_pages), *([v_pages_t] * n_pages))


def _mem_kv_kernel(m_ref, g_ref, w_ref, k_ref, v_ref, kbf_ref, vbf_ref):
    kv = _dot(_rms(m_ref[...], g_ref[...]).astype(BF16), w_ref[...])
    k, v = kv[:, :D_MEM_ATTN], kv[:, D_MEM_ATTN:]
    k_ref[...] = k
    v_ref[...] = v
    kbf_ref[...] = k.astype(BF16)
    vbf_ref[...] = v.astype(BF16)


def _mem_kv(mem, g, w):
    b, n, d = mem.shape
    blk = lambda c: pl.BlockSpec((None, n, c), lambda bi: (bi, 0, 0))
    const = lambda shape: pl.BlockSpec(shape, lambda bi: (0,) * len(shape))
    sds = lambda dt: jax.ShapeDtypeStruct((b, n, D_MEM_ATTN), dt)
    return pl.pallas_call(
        _mem_kv_kernel, grid=(b,), in_specs=[blk(d), const((1, d)), const(w.shape)],
        out_specs=(blk(D_MEM_ATTN),) * 4, out_shape=(sds(F32), sds(F32), sds(BF16), sds(BF16)),
        compiler_params=_cparams(("parallel",)), name="mem_kv",
    )(mem, g, w)


def _outproj_q(x, a, c, wo_ref, gq_ref, wcq_ref):
    x1 = x + _dot(a, wo_ref[0:D_ATTN, :]) + _dot(c, wo_ref[D_ATTN:, :])
    q = _dot(_rms(x1, gq_ref[...]).astype(BF16), wcq_ref[...]) * (MEM_HEAD_DIM ** -0.5)
    return x1, q.astype(BF16)


def _ffn_act(g, v):
    return (g * _sigmoid(g) * v).astype(BF16)


def _post_prompt_kernel(x_ref, a_ref, c_ref, mk_ref, mv_ref, wo_ref, gq_ref, wcq_ref, wco_ref, gf_ref,
                        wup_ref, fdw_ref, fdb_ref, wdn_ref, gfin_ref, y_ref, ffn_ref, ext_sc, *, tm, n_chunks):
    d_ff = wdn_ref.shape[0]
    hc = d_ff // n_chunks
    pad = SUBLANES

    @pl.when(pl.program_id(1) == 0)
    def _fresh_sequence():
        ext_sc[0:pad, :] = jnp.zeros((pad, ext_sc.shape[1]), F32)

    x1, qb = _outproj_q(x_ref[...], a_ref[...], c_ref[...], wo_ref, gq_ref, wcq_ref)
    head_of_lane = lax.broadcasted_iota(I32, (1, D_MEM_ATTN), 1) // MEM_HEAD_DIM
    mk = mk_ref[...]
    mv = mv_ref[...]
    o = jnp.zeros((tm, D_MEM_ATTN), F32)
    for h in range(MEM_HEADS):
        mine = head_of_lane == h
        lg = _dot_nt(jnp.where(mine, qb, jnp.zeros_like(qb)), mk)
        e = jnp.exp(lg - jnp.max(lg, axis=-1, keepdims=True))
        p = e * (1.0 / jnp.sum(e, axis=-1, keepdims=True))
        o = o + _dot(p.astype(BF16), jnp.where(mine, mv, jnp.zeros_like(mv)))
    x2 = x1 + _dot(o.astype(BF16), wco_ref[...])

    hn = _rms(x2, gf_ref[...]).astype(BF16)
    for cc in range(2 * n_chunks):
        ext_sc[pad:pad + tm, cc * hc:(cc + 1) * hc] = _dot(hn, wup_ref[:, cc * hc:(cc + 1) * hc])
    ffn_ref[...] = ext_sc[pad + tm - 2:pad + tm, :]

    def conv(lo):
        sl = slice(lo, lo + hc)
        out = fdb_ref[:, sl]
        for j in range(FFN_CONV_WIDTH):
            out = out + fdw_ref[j:j + 1, sl] * ext_sc[pad - 2 + j:pad - 2 + j + tm, sl]
        return out

    acc = jnp.zeros((tm, x_ref.shape[-1]), F32)
    for cc in range(n_chunks):
        acc = acc + _dot(_ffn_act(conv(cc * hc), conv(d_ff + cc * hc)), wdn_ref[cc * hc:(cc + 1) * hc, :])
    ext_sc[0:pad, :] = ext_sc[tm:tm + pad, :]
    y_ref[...] = _rms(x2 + acc, gfin_ref[...])


def _post_prompt(x, a, c, mk, mv, lw, gfin, tm):
    b, s, d = x.shape
    d_ff = lw['w_down'].shape[0]
    n_mem = mk.shape[1]
    tile = lambda width: pl.BlockSpec((None, tm, width), lambda bi, si: (bi, si, 0))
    const = lambda arr: pl.BlockSpec(arr.shape, lambda bi, si: (0,) * arr.ndim, pipeline_mode=pl.Buffered(1))
    per_b = lambda r, cdim: pl.BlockSpec((None, r, cdim), lambda bi, si: (bi, 0, 0))
    weights = [lw['w_o'], lw['norm_mem_q'], lw['w_cq'], lw['w_co'], lw['norm_ffn'], lw['w_up'],
               lw['ffn_dw'], lw['ffn_db'], lw['w_down'], gfin]
    return pl.pallas_call(
        functools.partial(_post_prompt_kernel, tm=tm, n_chunks=2), grid=(b, s // tm),
        in_specs=[tile(d), tile(D_ATTN), tile(a.shape[-1]), per_b(n_mem, D_MEM_ATTN), per_b(n_mem, D_MEM_ATTN)]
                 + [const(w) for w in weights],
        out_specs=(tile(d), per_b(FFN_CONV_WIDTH - 1, 2 * d_ff)),
        out_shape=(jax.ShapeDtypeStruct((b, s, d), F32), jax.ShapeDtypeStruct((b, FFN_CONV_WIDTH - 1, 2 * d_ff), F32)),
        scratch_shapes=[pltpu.VMEM((tm + SUBLANES, 2 * d_ff), F32)],
        compiler_params=_cparams(("arbitrary", "arbitrary")), name="post_prompt",
    )(x, a, c, mk, mv, *weights)


def _sample_outproj_kernel(x_ref, a_ref, c_ref, wo_ref, gq_ref, wcq_ref, x1_ref, q_ref):
    x1, qb = _outproj_q(x_ref[...], a_ref[...].astype(BF16), c_ref[...], wo_ref, gq_ref, wcq_ref)
    x1_ref[...] = x1
    q_ref[...] = qb.astype(F32)


def _sample_outproj(x, a, c, lw):
    return pl.pallas_call(
        _sample_outproj_kernel,
        out_shape=(jax.ShapeDtypeStruct(x.shape, F32), jax.ShapeDtypeStruct((x.shape[0], D_MEM_ATTN), F32)),
        compiler_params=pltpu.CompilerParams(vmem_limit_bytes=VMEM_LIMIT_BYTES), name="sample_outproj",
    )(x, a, c, lw['w_o'], lw['norm_mem_q'], lw['w_cq'])


def _sample_mem_attn_kernel(q_ref, mkt_ref, mvt_ref, o_ref, *, bb):
    dm, n_mem = mkt_ref.shape[1:]
    heads = range(MEM_HEADS)
    rows_of = lambda h: slice(h * MEM_HEAD_DIM, (h + 1) * MEM_HEAD_DIM)

    def body(r, carry):
        q = q_ref[pl.ds(r, 1), :]
        prod = mkt_ref[r].astype(BF16).astype(F32) * jnp.broadcast_to(q, (n_mem, dm)).T
        lg = jnp.concatenate([jnp.sum(prod[rows_of(h), :], axis=0, keepdims=True) for h in heads], axis=0)
        e = jnp.exp(lg - jnp.max(lg, axis=1, keepdims=True))
        pb = (e * (1.0 / jnp.sum(e, axis=1, keepdims=True))).astype(BF16).astype(F32)
        pe = jnp.concatenate([jnp.broadcast_to(pb[h:h + 1, :], (MEM_HEAD_DIM, n_mem)) for h in heads], axis=0)
        col = jnp.sum(pe * mvt_ref[r], axis=1, keepdims=True)
        o_ref[pl.ds(r, 1), :] = jnp.broadcast_to(col, (dm, LANES)).T[0:1, :]
        return carry

    lax.fori_loop(0, bb, body, 0)


def _sample_mem_attn(q, mk, mv, bb):
    db, dm, n_mem = mk.shape
    return pl.pallas_call(
        functools.partial(_sample_mem_attn_kernel, bb=bb), grid=(db // bb,),
        in_specs=[pl.BlockSpec((bb, dm), lambda i: (i, 0)),
                  pl.BlockSpec((bb, dm, n_mem), lambda i: (i, 0, 0)),
                  pl.BlockSpec((bb, dm, n_mem), lambda i: (i, 0, 0))],
        out_specs=pl.BlockSpec((bb, dm), lambda i: (i, 0)),
        out_shape=jax.ShapeDtypeStruct((db, dm), F32),
        compiler_params=_cparams(("parallel",)), name="sample_mem_attn",
    )(q, mk, mv)


def _sample_ffn_kernel(x1_ref, o_ref, h0_ref, h1_ref, wco_ref, gf_ref, wup_ref, fdw_ref, fdb_ref, wdn_ref,
                       gfin_ref, y_ref, up_ref, *, n_chunks):
    d_ff = wdn_ref.shape[0]
    hc = d_ff // n_chunks
    x2 = x1_ref[...] + _dot(o_ref[...].astype(BF16), wco_ref[...])
    hn = _rms(x2, gf_ref[...]).astype(BF16)

    def conv(lo):
        sl = slice(lo, lo + hc)
        up = _dot(hn, wup_ref[:, sl])
        up_ref[:, sl] = up
        return fdb_ref[:, sl] + fdw_ref[0:1, sl] * h0_ref[:, sl] + fdw_ref[1:2, sl] * h1_ref[:, sl] + fdw_ref[2:3, sl] * up

    acc = jnp.zeros(x1_ref.shape, F32)
    for cc in range(n_chunks):
        acc = acc + _dot(_ffn_act(conv(cc * hc), conv(d_ff + cc * hc)), wdn_ref[cc * hc:(cc + 1) * hc, :])
    y_ref[...] = _rms(x2 + acc, gfin_ref[...])


def _sample_ffn(x1, o, h0, h1, lw, gfin):
    d_ff = lw['w_down'].shape[0]
    return pl.pallas_call(
        functools.partial(_sample_ffn_kernel, n_chunks=2),
        out_shape=(jax.ShapeDtypeStruct(x1.shape, F32), jax.ShapeDtypeStruct((x1.shape[0], 2 * d_ff), F32)),
        compiler_params=pltpu.CompilerParams(vmem_limit_bytes=VMEM_LIMIT_BYTES), name="sample_ffn",
    )(x1, o, h0, h1, lw['w_co'], lw['norm_ffn'], lw['w_up'], lw['ffn_dw'], lw['ffn_db'], lw['w_down'], gfin)


def _prompt_layer(x, mem, rel_table, lw, gfin):
    b, s, _ = x.shape
    kt, vt, kit, kiwi, q_bf, k_bf, vt_bf, qi_bf, ki_bf, u = _inproj(x, lw['norm_mix'], lw['w_cat'], tm=512)
    c_bf = _conv_prompt(u, lw['conv_dw'], lw['conv_db'], lw['conv_ln_g'], lw['conv_ln_b'], tm=512)
    a_bf = _dsa_prompt(rel_table, q_bf, qi_bf, kiwi, ki_bf, k_bf, vt_bf, tq=256)
    mk, mv, mk_bf, mv_bf = _mem_kv(mem, lw['norm_mem_kv'], lw['w_ckv'])
    y, ffn_state = _post_prompt(x, a_bf, c_bf, mk_bf, mv_bf, lw, gfin, tm=256)
    heads_last = lambda t: jnp.transpose(t.reshape(b, N_HEADS, HEAD_DIM, s), (0, 3, 1, 2))
    return (y, heads_last(kt), heads_last(vt), jnp.swapaxes(kit, 1, 2), u[:, s - (CONV_WIDTH - 1):],
            ffn_state, mk, mv)


def _sample_layer(x, cache_k, cache_v, cache_idx_k, page_table, cache_conv, cache_ffn, mem_k, mem_v,
                  rel_table, lw, gfin):
    db = x.shape[0]
    n_pages = page_table.shape[1]
    ps = cache_k.shape[1]
    xs = x.reshape(1, db, -1)
    kt, vt, _, kiwi, q_bf, _, _, qi_bf, _, u = (t[0] for t in _inproj(xs, lw['norm_mix'], lw['w_cat'], tm=db))
    k, v = kt.T, vt.T
    c_bf = _conv_sample(jnp.swapaxes(cache_conv, 0, 1), u, lw['conv_dw'], lw['conv_db'],
                        lw['conv_ln_g'], lw['conv_ln_b'])
    ki_new = kiwi[:, :IDX_DIM]
    w_idx = kiwi[:, IDX_DIM:IDX_DIM + N_IDX_HEADS]
    pages_t = lambda c: jnp.moveaxis(c.reshape(c.shape[0], ps, -1), 1, 2)
    scores, snew = _dsa_sample_scores(
        page_table, qi_bf.astype(F32).reshape(db, N_IDX_HEADS, IDX_DIM), w_idx.reshape(db, N_IDX_HEADS, 1),
        ki_new.reshape(db, 1, IDX_DIM), pages_t(cache_idx_k), rows=4)
    sel, sel_new = _dsa_sample_select(scores.reshape(db, n_pages * ps), snew.reshape(db, LANES))
    a = _dsa_sample_attend(
        page_table, rel_table, q_bf.astype(F32).reshape(db, 1, D_ATTN), k.reshape(db, 1, D_ATTN),
        v.reshape(db, 1, D_ATTN), sel.reshape(db, 1, n_pages * ps), sel_new.reshape(db, 1, LANES),
        pages_t(cache_k), pages_t(cache_v)).reshape(db, D_ATTN)
    x1, q_mem = _sample_outproj(x.reshape(db, -1), a, c_bf, lw)
    mem_t = lambda m: jnp.moveaxis(m.reshape(db, -1, D_MEM_ATTN), 1, 2)
    o = _sample_mem_attn(q_mem, mem_t(mem_k), mem_t(mem_v), bb=16)
    y, up = _sample_ffn(x1, o, cache_ffn[:, 0], cache_ffn[:, 1], lw, gfin)
    conv_state = jnp.concatenate([cache_conv[:, 1:], u[:, None, :]], axis=1)
    ffn_state = jnp.concatenate([cache_ffn[:, 1:], up[:, None, :]], axis=1)
    return y, k, v, ki_new, conv_state, ffn_state


def _layer_weights(l, norm_mix, w_in, conv_dw, conv_db, conv_ln_g, conv_ln_b, w_o, norm_mem_q, norm_mem_kv,
                   w_cq, w_ckv, w_co, norm_ffn, w_up, ffn_dw, ffn_db, w_down):
    n_main = 3 * D_ATTN + N_IDX_HEADS * IDX_DIM
    n_small = IDX_DIM + N_IDX_HEADS
    w = w_in[l]
    w_cat = jnp.concatenate(
        [w[:, :n_main], w[:, n_main + n_small:], w[:, n_main:n_main + n_small],
         jnp.zeros((w.shape[0], LANES - n_small), w.dtype)], axis=1).astype(BF16)
    row = lambda a: a[l].reshape(1, -1)
    return dict(
        norm_mix=row(norm_mix), w_cat=w_cat, conv_dw=conv_dw[l], conv_db=row(conv_db),
        conv_ln_g=row(conv_ln_g), conv_ln_b=row(conv_ln_b), w_o=w_o[l].astype(BF16),
        norm_mem_q=row(norm_mem_q), norm_mem_kv=row(norm_mem_kv), w_cq=w_cq[l].astype(BF16),
        w_ckv=w_ckv[l].astype(BF16), w_co=w_co[l].astype(BF16), norm_ffn=row(norm_ffn),
        w_up=w_up[l].astype(BF16), ffn_dw=ffn_dw[l], ffn_db=row(ffn_db), w_down=w_down[l].astype(BF16))


def kernel(x_prompt, x_sample, mem_prompt, cache_k, cache_v, cache_idx_k, page_table, cache_conv, cache_ffn,
           cache_mem_k, cache_mem_v, rel_bias, norm_mix, w_in, conv_dw, conv_db, conv_ln_g, conv_ln_b, w_o,
           norm_mem_q, norm_mem_kv, w_cq, w_ckv, w_co, norm_ffn, w_up, ffn_dw, ffn_db, w_down, norm_final):
    depth = w_in.shape[0]
    b, s, d = x_prompt.shape
    db = x_sample.shape[0]
    assert x_sample.shape[1] == 1
    yp, ys = x_prompt, x_sample
    outs = [[] for _ in range(12)]
    ones = jnp.ones((1, d), F32)
    for l in range(depth):
        lw = _layer_weights(l, norm_mix, w_in, conv_dw, conv_db, conv_ln_g, conv_ln_b, w_o, norm_mem_q,
                            norm_mem_kv, w_cq, w_ckv, w_co, norm_ffn, w_up, ffn_dw, ffn_db, w_down)
        assert depth == 1, "final-norm fusion below assumes a single layer"
        gfin = norm_final.reshape(1, d) if l == depth - 1 else ones
        yp, kp, vp, kip, cp, fp, mkp, mvp = _prompt_layer(yp, mem_prompt, rel_bias, lw, gfin)
        ys, k_s, v_s, ki_s, c_s, f_s = _sample_layer(
            ys, cache_k[l], cache_v[l], cache_idx_k[l], page_table, cache_conv[l], cache_ffn[l],
            cache_mem_k[l], cache_mem_v[l], rel_bias, lw, gfin)
        ys = ys.reshape(db, 1, d)
        vals = (kp.reshape(b, s, N_HEADS, HEAD_DIM), vp.reshape(b, s, N_HEADS, HEAD_DIM), kip,
                k_s.reshape(db, 1, N_HEADS, HEAD_DIM), v_s.reshape(db, 1, N_HEADS, HEAD_DIM),
                ki_s.reshape(db, 1, IDX_DIM), cp, c_s, fp, f_s,
                mkp.reshape(b, -1, MEM_HEADS, MEM_HEAD_DIM), mvp.reshape(b, -1, MEM_HEADS, MEM_HEAD_DIM))
        for lst, val in zip(outs, vals):
            lst.append(val)
    return (yp, ys) + tuple(jnp.stack(lst) for lst in outs)
```

```python
import functools
import math

import jax
import jax.numpy as jnp
from jax import lax
from jax.experimental import pallas as pl
from jax.experimental.pallas import tpu as pltpu

F32, BF16, I32, I16 = jnp.float32, jnp.bfloat16, jnp.int32, jnp.int16

N_HEADS = 8
HEAD_DIM = 64
D_ATTN = N_HEADS * HEAD_DIM
N_IDX_HEADS = 8
IDX_DIM = 64
TOPK_MAX = 256
CONV_WIDTH = 31
FFN_CONV_WIDTH = 3
MEM_HEADS = 4
MEM_HEAD_DIM = 64
D_MEM_ATTN = MEM_HEADS * MEM_HEAD_DIM
REL_BUCKETS = 32
REL_MAX_DIST = 128
RMS_EPS = 1e-6
LN_EPS = 1e-5
NEG_INF = -1e30
INT_MIN = -(2 ** 31)
HALF_BIAS = 2 ** 15
LANES = 128
SUBLANES = 8
VMEM_LIMIT_BYTES = 56 * 1024 * 1024
CONV_HALO = 32


def _cparams(sem):
    return pltpu.CompilerParams(dimension_semantics=sem, vmem_limit_bytes=VMEM_LIMIT_BYTES)


def _dot(a, b):
    return jnp.dot(a, b, preferred_element_type=F32)


def _dot_nt(a, b):
    return lax.dot_general(a, b, (((1,), (1,)), ((), ())), preferred_element_type=F32)


def _rms(x, g):
    return x * lax.rsqrt(jnp.mean(x * x, -1, keepdims=True) + RMS_EPS) * g


def _sigmoid(x):
    return 1.0 / (1.0 + jnp.exp(-x))


def _sort_key(x):
    b = lax.bitcast_convert_type(x, I32)
    b = jnp.where(b == INT_MIN, 0, b)
    return b ^ ((b >> 31) & 0x7FFFFFFF)


def _rel_bucket(n):
    n = jnp.maximum(n, 0)
    max_exact = REL_BUCKETS // 2
    nf = jnp.maximum(n, 1).astype(F32)
    large = max_exact + (jnp.log(nf / max_exact) / math.log(REL_MAX_DIST / max_exact)
                         * (REL_BUCKETS - max_exact)).astype(I32)
    large = jnp.minimum(large, REL_BUCKETS - 1)
    return jnp.where(n < max_exact, n, large)


def _split_bf16(x):
    hi = x.astype(BF16)
    lo = (x - hi.astype(F32)).astype(BF16)
    return hi, lo


def _head_select(n_rows, n_cols, rows_per_head, by_row):
    r = lax.broadcasted_iota(I32, (n_rows, n_cols), 0)
    c = lax.broadcasted_iota(I32, (n_rows, n_cols), 1)
    hit = (r // rows_per_head == c) if by_row else (c // rows_per_head == r)
    return jnp.where(hit, 1.0, 0.0).astype(BF16)


def _inproj_kernel(x_ref, g_ref, w_ref, kt_ref, vt_ref, kit_ref, kiwi_ref, qbf_ref, kbf_ref, vtbf_ref,
                   qibf_ref, kibf_ref, u_ref):
    hb = _rms(x_ref[...], g_ref[...]).astype(BF16)

    def mm(lo, hi):
        return _dot(hb, w_ref[:, lo:hi])

    qbf_ref[...] = (mm(0, 512) * (HEAD_DIM ** -0.5)).astype(BF16)
    k = mm(512, 1024)
    kt_ref[...] = k.T
    kbf_ref[...] = k.astype(BF16)
    vt = mm(1024, 1536).T
    vt_ref[...] = vt
    vtbf_ref[...] = vt.astype(BF16)
    qibf_ref[...] = (mm(1536, 2048) * (IDX_DIM ** -0.5)).astype(BF16)
    u_ref[...] = mm(2048, 2560) * _sigmoid(mm(2560, 3072))
    kw = mm(3072, 3200)
    kiwi_ref[...] = kw
    kit_ref[...] = kw.T[:IDX_DIM, :]
    kibf_ref[...] = kw[:, :IDX_DIM].astype(BF16)


def _inproj(x, g, w_cat, tm):
    b, s, d = x.shape
    n_w = w_cat.shape[1]
    row = lambda width: pl.BlockSpec((None, tm, width), lambda bi, si: (bi, si, 0))
    const = lambda shape: pl.BlockSpec(shape, lambda bi, si: (0,) * len(shape))
    col = lambda height: pl.BlockSpec((None, height, tm), lambda bi, si: (bi, 0, si))
    out_shape = (
        jax.ShapeDtypeStruct((b, D_ATTN, s), F32),
        jax.ShapeDtypeStruct((b, D_ATTN, s), F32),
        jax.ShapeDtypeStruct((b, IDX_DIM, s), F32),
        jax.ShapeDtypeStruct((b, s, LANES), F32),
        jax.ShapeDtypeStruct((b, s, D_ATTN), BF16),
        jax.ShapeDtypeStruct((b, s, D_ATTN), BF16),
        jax.ShapeDtypeStruct((b, D_ATTN, s), BF16),
        jax.ShapeDtypeStruct((b, s, D_ATTN), BF16),
        jax.ShapeDtypeStruct((b, s, IDX_DIM), BF16),
        jax.ShapeDtypeStruct((b, s, D_ATTN), F32),
    )
    out_specs = (col(D_ATTN), col(D_ATTN), col(IDX_DIM), row(LANES), row(D_ATTN), row(D_ATTN),
                 col(D_ATTN), row(D_ATTN), row(IDX_DIM), row(D_ATTN))
    return pl.pallas_call(
        _inproj_kernel, grid=(b, s // tm),
        in_specs=[row(d), const((1, d)), const((d, n_w))],
        out_specs=out_specs, out_shape=out_shape,
        compiler_params=_cparams(("parallel", "parallel")), name="inproj",
    )(x, g, w_cat)


def _conv_ln_silu(acc, g, b):
    mu = jnp.mean(acc, -1, keepdims=True)
    xc = acc - mu
    var = jnp.mean(xc * xc, -1, keepdims=True)
    y = xc * lax.rsqrt(var + LN_EPS) * g + b
    return y * _sigmoid(y)


def _conv_prompt_kernel(u_ref, halo_ref, w_ref, b_ref, g_ref, bb_ref, c_ref, ext_sc, shift_sc, *, tm, rc):
    i = pl.program_id(1)
    ext_sc[0:CONV_HALO, :] = jnp.where(i == 0, 0.0, halo_ref[...])
    ext_sc[CONV_HALO:CONV_HALO + tm, :] = u_ref[...]
    first = CONV_HALO - (CONV_WIDTH - 1)
    span = tm + CONV_HALO - SUBLANES
    for mis in range(1, SUBLANES):
        shift_sc[mis, 0:span, :] = ext_sc[mis:mis + span, :]
    for r in range(tm // rc):
        acc = jnp.zeros((rc, u_ref.shape[-1]), F32) + b_ref[...]
        for j in range(CONV_WIDTH):
            mis = (first + j) % SUBLANES
            lo = r * rc + (first + j) - mis
            rows = ext_sc[lo:lo + rc, :] if mis == 0 else shift_sc[mis, lo:lo + rc, :]
            acc = acc + w_ref[j:j + 1, :] * rows
        c_ref[r * rc:(r + 1) * rc, :] = _conv_ln_silu(acc, g_ref[...], bb_ref[...]).astype(BF16)


def _conv_prompt(u, w, b, g, bb, tm):
    bsz, s, c = u.shape
    hb = tm // CONV_HALO
    const = lambda shape: pl.BlockSpec(shape, lambda bi, si: (0,) * len(shape))
    return pl.pallas_call(
        functools.partial(_conv_prompt_kernel, tm=tm, rc=64), grid=(bsz, s // tm),
        in_specs=[pl.BlockSpec((None, tm, c), lambda bi, si: (bi, si, 0)),
                  pl.BlockSpec((None, CONV_HALO, c), lambda bi, si: (bi, jnp.maximum(si * hb - 1, 0), 0)),
                  const((CONV_WIDTH, c)), const((1, c)), const((1, c)), const((1, c))],
        out_specs=pl.BlockSpec((None, tm, c), lambda bi, si: (bi, si, 0)),
        out_shape=jax.ShapeDtypeStruct((bsz, s, c), BF16),
        scratch_shapes=[pltpu.VMEM((tm + CONV_HALO, c), F32),
                        pltpu.VMEM((SUBLANES, tm + CONV_HALO - SUBLANES, c), F32)],
        compiler_params=_cparams(("parallel", "parallel")), name="conv_prompt",
    )(u, u, w, b, g, bb)


def _conv_sample_kernel(hist_ref, u_ref, w_ref, b_ref, g_ref, bb_ref, c_ref):
    acc = w_ref[CONV_WIDTH - 1:CONV_WIDTH, :] * u_ref[...] + b_ref[...]
    for j in range(CONV_WIDTH - 1):
        acc = acc + w_ref[j:j + 1, :] * hist_ref[j]
    c_ref[...] = _conv_ln_silu(acc, g_ref[...], bb_ref[...]).astype(BF16)


def _conv_sample(hist_t, u, w, b, g, bb):
    return pl.pallas_call(
        _conv_sample_kernel, out_shape=jax.ShapeDtypeStruct(u.shape, BF16),
        compiler_params=pltpu.CompilerParams(vmem_limit_bytes=VMEM_LIMIT_BYTES), name="conv_sample",
    )(hist_t, u, w, b, g, bb)


def _dsa_prompt_kernel(tab_ref, q_ref, qi_ref, kiwi_ref, ki_ref, k_ref, vt_ref, a_ref,
                       keys_sc, hi_sc, lo_sc, qz_sc, qit_sc, wt_sc, bias_sc, madd_sc, s0_sc, s1_sc, acc_sc,
                       *, tq, topk, unroll):
    kc = tq
    b = pl.program_id(0)
    i = pl.program_id(1)

    @pl.when((b == 0) & (i == 0))
    def _build_bias():
        kk = lax.broadcasted_iota(I32, (kc, tq), 0)
        qq = lax.broadcasted_iota(I32, (kc, tq), 1)
        for d in range(3):
            n = d * tq + qq - kk
            bucket = _rel_bucket(n)
            for h in range(N_HEADS):
                val = jnp.zeros((kc, tq), F32)
                for bk in range(REL_BUCKETS):
                    val = jnp.where(bucket == bk, tab_ref[bk, h], val)
                if d == 0:
                    val = jnp.where(n >= 0, val, NEG_INF)
                bias_sc[d, h] = val
        for buf in (s0_sc, s1_sc):
            buf[buf.shape[0] - kc:, :] = jnp.full((kc, tq), NEG_INF, F32)

    qt = q_ref[...].astype(F32).T
    zeros = jnp.zeros((HEAD_DIM, tq), BF16)
    for h in range(N_HEADS):
        blk = qt[h * HEAD_DIM:(h + 1) * HEAD_DIM, :].astype(BF16)
        qz_sc[h] = jnp.concatenate([blk, zeros] if h % 2 == 0 else [zeros, blk], axis=0)
    qit_sc[...] = qi_ref[...].astype(F32).T.astype(BF16)
    wt_sc[...] = kiwi_ref[...].T[IDX_DIM:IDX_DIM + N_IDX_HEADS, :] * (N_IDX_HEADS ** -0.5)

    nc = i + 1
    qpos = i * tq + lax.broadcasted_iota(I32, (kc, tq), 1)

    def chunk(c):
        return pl.ds(pl.multiple_of(c * kc, kc), kc)

    def score_body(c, carry):
        kchunk = ki_ref[chunk(c), :]
        acc = jnp.zeros((kc, tq), F32)
        for h in range(N_IDX_HEADS):
            s = _dot(kchunk, qit_sc[h * IDX_DIM:(h + 1) * IDX_DIM, :])
            acc = acc + wt_sc[h:h + 1, :] * jnp.maximum(s, 0.0)
        kpos = c * kc + lax.broadcasted_iota(I32, (kc, tq), 0)
        key = jnp.where(kpos <= qpos, _sort_key(acc), INT_MIN)
        keys_sc[chunk(c), :] = key
        hi_sc[chunk(c), :] = (key >> 16).astype(I16)
        lo_sc[chunk(c), :] = ((key & 0xFFFF) - HALF_BIAS).astype(I16)
        return carry

    lax.fori_loop(0, nc, score_body, 0)

    def count(pred):
        def body(c, cnt):
            ind = jnp.where(pred(keys_sc[chunk(c), :]), 1, 0).astype(I32)
            return cnt + ind.reshape(kc // SUBLANES, SUBLANES, tq).sum(axis=0)
        cnt8 = lax.fori_loop(0, nc, body, jnp.zeros((SUBLANES, tq), I32))
        return jnp.sum(cnt8, axis=0, keepdims=True)

    def count16(ref, pred):
        rows = 2 * SUBLANES

        def body(c, cnt):
            ind = jnp.where(pred(ref[chunk(c), :]), jnp.int16(1), jnp.int16(0))
            for j in range(kc // rows):
                cnt = cnt + ind[j * rows:(j + 1) * rows, :]
            return cnt
        cnt16 = lax.fori_loop(0, nc, body, jnp.zeros((rows, tq), I16))
        return jnp.sum(cnt16.astype(I32), axis=0, keepdims=True)

    def kth_largest16(ref, kth):
        def bit_body(it, t):
            tc = t | jnp.left_shift(jnp.int32(1), 15 - it)
            tcs = (tc - HALF_BIAS).astype(I16)
            return jnp.where(count16(ref, lambda x: x >= tcs) >= kth, tc, t)
        return lax.fori_loop(0, 16, bit_body, jnp.zeros((1, tq), I32))

    t_hi = kth_largest16(hi_sc, topk) - HALF_BIAS
    t_hi16 = t_hi.astype(I16)
    need_lo = topk - count16(hi_sc, lambda x: x > t_hi16)

    def low_body(c, carry):
        lo_sc[chunk(c), :] = jnp.where(hi_sc[chunk(c), :] == t_hi16, lo_sc[chunk(c), :], jnp.int16(-HALF_BIAS))
        return carry

    lax.fori_loop(0, nc, low_body, 0)
    thr = t_hi * (2 * HALF_BIAS) + kth_largest16(lo_sc, need_lo)

    tie = (count(lambda k: k >= thr) > topk) & (thr > INT_MIN)

    @pl.when(jnp.max(tie.astype(I32)) > 0)
    def _resolve_ties():
        need = (topk - count(lambda k: k > thr)).astype(F32)
        r = lax.broadcasted_iota(I32, (kc, kc), 0)
        cc = lax.broadcasted_iota(I32, (kc, kc), 1)
        lower = jnp.where(r >= cc, 1.0, 0.0).astype(BF16)

        def fix_body(c, seen):
            k = keys_sc[chunk(c), :]
            eq = k == thr
            eqf = jnp.where(eq, 1.0, 0.0)
            rank = _dot(lower, eqf.astype(BF16)) + seen
            drop = eq & (rank > need) & tie
            keys_sc[chunk(c), :] = jnp.where(drop, k - 1, k)
            return seen + jnp.sum(eqf, axis=0, keepdims=True)

        lax.fori_loop(0, nc, fix_body, jnp.zeros((1, tq), F32))

    def mask_body(c, carry):
        madd_sc[chunk(c), :] = jnp.where(keys_sc[chunk(c), :] >= thr, 0.0, NEG_INF)
        return carry

    lax.fori_loop(0, nc, mask_body, 0)

    def fold8(x, op):
        return op(x.reshape(kc // SUBLANES, SUBLANES, tq), axis=0)

    n_groups = (nc + unroll - 1) // unroll
    masked_chunk = s0_sc.shape[0] // kc - 1
    s_bufs = (s0_sc, s1_sc)

    def logits_step(h, c, mx):
        pair = (h // 2) * 2 * HEAD_DIM
        s = (_dot(k_ref[chunk(c), pair:pair + 2 * HEAD_DIM], qz_sc[h]) + madd_sc[chunk(c), :]
             + bias_sc[jnp.minimum(i - c, 2), h])
        s_bufs[h % 2][chunk(c), :] = s
        return jnp.maximum(mx, fold8(s, jnp.max))

    def pv_step(h, c, m, l8, acc):
        p = jnp.exp(s_bufs[h % 2][chunk(jnp.where(c <= i, c, masked_chunk)), :] - m)
        hs = slice(h * HEAD_DIM, (h + 1) * HEAD_DIM)
        return l8 + fold8(p, jnp.sum), acc + _dot(vt_ref[hs, chunk(jnp.minimum(c, i))], p.astype(BF16))

    def head_loop(h_logits, h_pv, m):
        def body(g, carry):
            mx, l8, acc = carry
            for u in range(unroll):
                c = g * unroll + u
                if h_logits is not None:
                    mx = logits_step(h_logits, jnp.minimum(c, i), mx)
                if h_pv is not None:
                    l8, acc = pv_step(h_pv, c, m, l8, acc)
            return mx, l8, acc
        return lax.fori_loop(0, n_groups, body, (jnp.full((SUBLANES, tq), -3e38, F32),
                                                 jnp.zeros((SUBLANES, tq), F32), jnp.zeros((HEAD_DIM, tq), F32)))

    m = None
    for h in range(N_HEADS + 1):
        mx, l8, acc = head_loop(h if h < N_HEADS else None, h - 1 if h > 0 else None, m)
        if h > 0:
            hs = slice((h - 1) * HEAD_DIM, h * HEAD_DIM)
            acc_sc[hs, :] = acc * (1.0 / jnp.sum(l8, axis=0, keepdims=True))
        m = jnp.max(mx, axis=0, keepdims=True)
    a_ref[...] = acc_sc[...].T.astype(BF16)


def _dsa_prompt(rel_table, q_bf, qi_bf, kiwi, ki_bf, k_bf, vt_bf, tq):
    b, s, _ = q_bf.shape
    assert s % tq == 0 and tq >= REL_MAX_DIST
    topk = min(TOPK_MAX, s // 4)
    tile = lambda width: pl.BlockSpec((None, tq, width), lambda bi, si: (bi, si, 0))
    whole = lambda rows, cols: pl.BlockSpec((None, rows, cols), lambda bi, si: (bi, 0, 0))
    return pl.pallas_call(
        functools.partial(_dsa_prompt_kernel, tq=tq, topk=topk, unroll=4), grid=(b, s // tq),
        in_specs=[pl.BlockSpec(memory_space=pltpu.SMEM), tile(D_ATTN), tile(D_ATTN), tile(LANES),
                  whole(s, IDX_DIM), whole(s, D_ATTN), whole(D_ATTN, s)],
        out_specs=tile(D_ATTN), out_shape=jax.ShapeDtypeStruct((b, s, D_ATTN), BF16),
        scratch_shapes=[
            pltpu.VMEM((s, tq), I32),
            pltpu.VMEM((s, tq), I16), pltpu.VMEM((s, tq), I16),
            pltpu.VMEM((N_HEADS, 2 * HEAD_DIM, tq), BF16),
            pltpu.VMEM((D_ATTN, tq), BF16),
            pltpu.VMEM((N_IDX_HEADS, tq), F32),
            pltpu.VMEM((3, N_HEADS, tq, tq), F32),
            pltpu.VMEM((s, tq), F32),
            pltpu.VMEM((s + tq, tq), F32), pltpu.VMEM((s + tq, tq), F32),
            pltpu.VMEM((D_ATTN, tq), F32),
        ],
        compiler_params=_cparams(("arbitrary", "arbitrary")), name="dsa_prompt",
    )(rel_table, q_bf, qi_bf, kiwi, ki_bf, k_bf, vt_bf)


def _dsa_sample_scores_kernel(pt_ref, qi_ref, w_ref, kinew_ref, *rest, n_pages, rows):
    pages, (sc_ref, snew_ref) = rest[:rows * n_pages], rest[rows * n_pages:]
    for r in range(rows):
        qh = qi_ref[r]
        w = w_ref[r] * (N_IDX_HEADS ** -0.5)
        qb = qh.astype(BF16)
        for p in range(n_pages):
            s = _dot(qb, pages[r * n_pages + p][...].astype(BF16))
            sc_ref[r, p:p + 1, :] = jnp.sum(w * jnp.maximum(s, 0.0), axis=0, keepdims=True)
        kn = kinew_ref[r].astype(BF16).astype(F32)
        sn = jnp.sum(qh * kn, axis=1, keepdims=True)
        tot = jnp.sum(w * jnp.maximum(sn, 0.0), axis=0, keepdims=True)
        snew_ref[r] = jnp.broadcast_to(tot, (1, LANES))


def _dsa_sample_scores(page_table, qi, w, ki_new, idx_pages_t, rows):
    db, n_pages = page_table.shape
    _, di, ps = idx_pages_t.shape
    assert db % rows == 0

    def page_spec(r, p):
        return pl.BlockSpec((None, di, ps), lambda bi, pt: (pt[(bi * rows + r) * n_pages + p], 0, 0))

    per_b = lambda r, c: pl.BlockSpec((rows, r, c), lambda bi, pt: (bi, 0, 0))
    grid_spec = pltpu.PrefetchScalarGridSpec(
        num_scalar_prefetch=1, grid=(db // rows,),
        in_specs=([per_b(N_IDX_HEADS, di), per_b(N_IDX_HEADS, 1), per_b(1, di)]
                  + [page_spec(r, p) for r in range(rows) for p in range(n_pages)]),
        out_specs=(per_b(n_pages, ps), per_b(1, LANES)))
    return pl.pallas_call(
        functools.partial(_dsa_sample_scores_kernel, n_pages=n_pages, rows=rows), grid_spec=grid_spec,
        out_shape=(jax.ShapeDtypeStruct((db, n_pages, ps), F32), jax.ShapeDtypeStruct((db, 1, LANES), F32)),
        compiler_params=_cparams(("arbitrary",)), name="dsa_sample_scores",
    )(page_table.reshape(-1), qi, w, ki_new, *([idx_pages_t] * (rows * n_pages)))


def _dsa_sample_select_kernel(sc_ref, snew_ref, sel_ref, selnew_ref, *, topk):
    keys = _sort_key(sc_ref[...])
    knew = _sort_key(snew_ref[...])[:, 0:1]
    rows, past = keys.shape

    def count(kmat, kn):
        return (jnp.sum(jnp.where(kmat, 1, 0).astype(I32), axis=1, keepdims=True)
                + jnp.where(kn, 1, 0).astype(I32))

    def bit_body(it, t):
        tc = t | jnp.left_shift(jnp.int32(1), 31 - it)
        tcs = tc ^ INT_MIN
        return jnp.where(count(keys >= tcs, knew >= tcs) >= topk, tc, t)

    thr = lax.fori_loop(0, 32, bit_body, jnp.zeros((rows, 1), I32)) ^ INT_MIN
    gt = keys > thr
    eq = keys == thr
    need = (topk - count(gt, knew > thr)).astype(F32)
    r = lax.broadcasted_iota(I32, (LANES, LANES), 0)
    c = lax.broadcasted_iota(I32, (LANES, LANES), 1)
    upper = jnp.where(r <= c, 1.0, 0.0).astype(BF16)
    seen = jnp.zeros((rows, 1), F32)
    for j in range(past // LANES):
        sl = slice(j * LANES, (j + 1) * LANES)
        eqf = jnp.where(eq[:, sl], 1.0, 0.0)
        rank = _dot(eqf.astype(BF16), upper) + seen
        sel_ref[:, sl] = jnp.where(gt[:, sl] | (eq[:, sl] & (rank <= need)), 1.0, 0.0)
        seen = seen + jnp.sum(eqf, axis=1, keepdims=True)
    keep_new = (knew > thr) | ((knew == thr) & (seen + 1.0 <= need))
    selnew_ref[...] = jnp.broadcast_to(jnp.where(keep_new, 1.0, 0.0), selnew_ref.shape)


def _dsa_sample_select(scores, snew):
    db, past = scores.shape
    topk = min(TOPK_MAX, (past + 1) // 4)
    return pl.pallas_call(
        functools.partial(_dsa_sample_select_kernel, topk=topk),
        out_shape=(jax.ShapeDtypeStruct((db, past), F32), jax.ShapeDtypeStruct((db, LANES), F32)),
        compiler_params=pltpu.CompilerParams(vmem_limit_bytes=VMEM_LIMIT_BYTES), name="dsa_sample_select",
    )(scores, snew)


def _dsa_sample_attend_kernel(pt_ref, tab_ref, q_ref, knew_ref, vnew_ref, sel_ref, selnew_ref, *rest,
                              n_pages, ps):
    kpages, vpages = rest[:n_pages], rest[n_pages:2 * n_pages]
    a_ref, bias_sc, lg_sc = rest[2 * n_pages:]
    past = n_pages * ps
    width = past + ps

    @pl.when(pl.program_id(0) == 0)
    def _build_bias():
        head = lax.broadcasted_iota(I32, (N_HEADS, 1), 0)
        pos = lax.broadcasted_iota(I32, (N_HEADS, width), 1)
        bucket = _rel_bucket(past - pos)
        val = jnp.zeros((N_HEADS, width), F32)
        for bk in range(REL_BUCKETS):
            col = jnp.zeros((N_HEADS, 1), F32)
            for h in range(N_HEADS):
                col = jnp.where(head == h, tab_ref[bk, h], col)
            val = jnp.where(bucket == bk, col, val)
        bias_sc[...] = val

    def as_columns(row):
        return jnp.broadcast_to(row, (ps, row.shape[1])).T

    qc = as_columns(q_ref[...])

    def head_logits(kt):
        prod = kt.astype(BF16).astype(F32) * qc
        return jnp.concatenate(
            [jnp.sum(prod[h * HEAD_DIM:(h + 1) * HEAD_DIM, :], axis=0, keepdims=True) for h in range(N_HEADS)], axis=0)

    for p in range(n_pages):
        lg_sc[:, p * ps:(p + 1) * ps] = head_logits(kpages[p][...])
    lg_sc[:, past:] = head_logits(as_columns(knew_ref[...]))
    lane = lax.broadcasted_iota(I32, (1, ps), 1)
    sel_all = jnp.concatenate([sel_ref[...], jnp.where(lane == 0, selnew_ref[...], 0.0)], axis=1)
    lg = jnp.where(sel_all > 0.0, lg_sc[...] + bias_sc[...], NEG_INF)
    e = jnp.exp(lg - jnp.max(lg, axis=1, keepdims=True))
    pb = (e * (1.0 / jnp.sum(e, axis=1, keepdims=True))).astype(BF16).astype(F32)

    acc = jnp.zeros((D_ATTN, ps), F32)
    for p in range(n_pages + 1):
        pe = jnp.concatenate(
            [jnp.broadcast_to(pb[h:h + 1, p * ps:(p + 1) * ps], (HEAD_DIM, ps)) for h in range(N_HEADS)], axis=0)
        acc = acc + pe * (vpages[p][...] if p < n_pages else as_columns(vnew_ref[...]))
    col = jnp.sum(acc, axis=1, keepdims=True)
    a_ref[...] = jnp.broadcast_to(col, (D_ATTN, ps)).T[0:1, :]


def _dsa_sample_attend(page_table, rel_table, q, k_new, v_new, sel, sel_new, k_pages_t, v_pages_t):
    db, n_pages = page_table.shape
    _, dk, ps = k_pages_t.shape
    assert ps == LANES and dk == D_ATTN
    page_spec = lambda p: pl.BlockSpec((None, dk, ps), lambda bi, pt: (pt[bi * n_pages + p], 0, 0))
    per_b = lambda r, c: pl.BlockSpec((None, r, c), lambda bi, pt: (bi, 0, 0))
    width = (n_pages + 1) * ps
    grid_spec = pltpu.PrefetchScalarGridSpec(
        num_scalar_prefetch=1, grid=(db,),
        in_specs=([pl.BlockSpec(memory_space=pltpu.SMEM), per_b(1, dk), per_b(1, dk), per_b(1, dk),
                   per_b(1, n_pages * ps), per_b(1, LANES)]
                  + [page_spec(p) for p in range(n_pages)] * 2),
        out_specs=per_b(1, dk),
        scratch_shapes=[pltpu.VMEM((N_HEADS, width), F32), pltpu.VMEM((N_HEADS, width), F32)])
    return pl.pallas_call(
        functools.partial(_dsa_sample_attend_kernel, n_pages=n_pages, ps=ps), grid_spec=grid_spec,
        out_shape=jax.ShapeDtypeStruct((db, 1, dk), F32),
        compiler_params=_cparams(("arbitrary",)), name="dsa_sample_attend",
    )(page_table.reshape(-1), rel_table, q, k_new, v_new, sel, sel_new,
      *([k_pages_t] * n_pages), *([v_pages_t] * n_pages))


def _mem_kv_kernel(m_ref, g_ref, w_ref, k_ref, v_ref, kbf_ref, vbf_ref):
    kv = _dot(_rms(m_ref[...], g_ref[...]).astype(BF16), w_ref[...])
    k, v = kv[:, :D_MEM_ATTN], kv[:, D_MEM_ATTN:]
    k_ref[...] = k
    v_ref[...] = v
    kbf_ref[...] = k.astype(BF16)
    vbf_ref[...] = v.astype(BF16)


def _mem_kv(mem, g, w):
    b, n, d = mem.shape
    blk = lambda c: pl.BlockSpec((None, n, c), lambda bi: (bi, 0, 0))
    const = lambda shape: pl.BlockSpec(shape, lambda bi: (0,) * len(shape))
    sds = lambda dt: jax.ShapeDtypeStruct((b, n, D_MEM_ATTN), dt)
    return pl.pallas_call(
        _mem_kv_kernel, grid=(b,), in_specs=[blk(d), const((1, d)), const(w.shape)],
        out_specs=(blk(D_MEM_ATTN),) * 4, out_shape=(sds(F32), sds(F32), sds(BF16), sds(BF16)),
        compiler_params=_cparams(("parallel",)), name="mem_kv",
    )(mem, g, w)


def _outproj_q(x, a, c, wo_ref, gq_ref, wcq_ref):
    x1 = x + _dot(a, wo_ref[0:D_ATTN, :]) + _dot(c, wo_ref[D_ATTN:, :])
    q = _dot(_rms(x1, gq_ref[...]).astype(BF16), wcq_ref[...]) * (MEM_HEAD_DIM ** -0.5)
    return x1, q.astype(BF16)


def _ffn_act(g, v):
    return (g * _sigmoid(g) * v).astype(BF16)


def _post_prompt_kernel(x_ref, a_ref, c_ref, mk_ref, mv_ref, wo_ref, gq_ref, wcq_ref, wco_ref, gf_ref,
                        wup_ref, fdw_ref, fdb_ref, wdn_ref, gfin_ref, y_ref, ffn_ref, ext_sc, *, tm, n_chunks):
    d_ff = wdn_ref.shape[0]
    hc = d_ff // n_chunks
    pad = SUBLANES

    @pl.when(pl.program_id(1) == 0)
    def _fresh_sequence():
        ext_sc[0:pad, :] = jnp.zeros((pad, ext_sc.shape[1]), F32)

    x1, qb = _outproj_q(x_ref[...], a_ref[...], c_ref[...], wo_ref, gq_ref, wcq_ref)
    head_of_lane = lax.broadcasted_iota(I32, (1, D_MEM_ATTN), 1) // MEM_HEAD_DIM
    mk = mk_ref[...]
    mv = mv_ref[...]
    o = jnp.zeros((tm, D_MEM_ATTN), F32)
    for h in range(MEM_HEADS):
        mine = head_of_lane == h
        lg = _dot_nt(jnp.where(mine, qb, jnp.zeros_like(qb)), mk)
        e = jnp.exp(lg - jnp.max(lg, axis=-1, keepdims=True))
        p = e * (1.0 / jnp.sum(e, axis=-1, keepdims=True))
        o = o + _dot(p.astype(BF16), jnp.where(mine, mv, jnp.zeros_like(mv)))
    x2 = x1 + _dot(o.astype(BF16), wco_ref[...])

    hn = _rms(x2, gf_ref[...]).astype(BF16)
    for cc in range(2 * n_chunks):
        ext_sc[pad:pad + tm, cc * hc:(cc + 1) * hc] = _dot(hn, wup_ref[:, cc * hc:(cc + 1) * hc])
    ffn_ref[...] = ext_sc[pad + tm - 2:pad + tm, :]

    def conv(lo):
        sl = slice(lo, lo + hc)
        out = fdb_ref[:, sl]
        for j in range(FFN_CONV_WIDTH):
            out = out + fdw_ref[j:j + 1, sl] * ext_sc[pad - 2 + j:pad - 2 + j + tm, sl]
        return out

    acc = jnp.zeros((tm, x_ref.shape[-1]), F32)
    for cc in range(n_chunks):
        acc = acc + _dot(_ffn_act(conv(cc * hc), conv(d_ff + cc * hc)), wdn_ref[cc * hc:(cc + 1) * hc, :])
    ext_sc[0:pad, :] = ext_sc[tm:tm + pad, :]
    y_ref[...] = _rms(x2 + acc, gfin_ref[...])


def _post_prompt(x, a, c, mk, mv, lw, gfin, tm):
    b, s, d = x.shape
    d_ff = lw['w_down'].shape[0]
    n_mem = mk.shape[1]
    tile = lambda width: pl.BlockSpec((None, tm, width), lambda bi, si: (bi, si, 0))
    const = lambda arr: pl.BlockSpec(arr.shape, lambda bi, si: (0,) * arr.ndim, pipeline_mode=pl.Buffered(1))
    per_b = lambda r, cdim: pl.BlockSpec((None, r, cdim), lambda bi, si: (bi, 0, 0))
    weights = [lw['w_o'], lw['norm_mem_q'], lw['w_cq'], lw['w_co'], lw['norm_ffn'], lw['w_up'],
               lw['ffn_dw'], lw['ffn_db'], lw['w_down'], gfin]
    return pl.pallas_call(
        functools.partial(_post_prompt_kernel, tm=tm, n_chunks=2), grid=(b, s // tm),
        in_specs=[tile(d), tile(D_ATTN), tile(a.shape[-1]), per_b(n_mem, D_MEM_ATTN), per_b(n_mem, D_MEM_ATTN)]
                 + [const(w) for w in weights],
        out_specs=(tile(d), per_b(FFN_CONV_WIDTH - 1, 2 * d_ff)),
        out_shape=(jax.ShapeDtypeStruct((b, s, d), F32), jax.ShapeDtypeStruct((b, FFN_CONV_WIDTH - 1, 2 * d_ff), F32)),
        scratch_shapes=[pltpu.VMEM((tm + SUBLANES, 2 * d_ff), F32)],
        compiler_params=_cparams(("arbitrary", "arbitrary")), name="post_prompt",
    )(x, a, c, mk, mv, *weights)


def _sample_outproj_kernel(x_ref, a_ref, c_ref, wo_ref, gq_ref, wcq_ref, x1_ref, q_ref):
    x1, qb = _outproj_q(x_ref[...], a_ref[...].astype(BF16), c_ref[...], wo_ref, gq_ref, wcq_ref)
    x1_ref[...] = x1
    q_ref[...] = qb.astype(F32)


def _sample_outproj(x, a, c, lw):
    return pl.pallas_call(
        _sample_outproj_kernel,
        out_shape=(jax.ShapeDtypeStruct(x.shape, F32), jax.ShapeDtypeStruct((x.shape[0], D_MEM_ATTN), F32)),
        compiler_params=pltpu.CompilerParams(vmem_limit_bytes=VMEM_LIMIT_BYTES), name="sample_outproj",
    )(x, a, c, lw['w_o'], lw['norm_mem_q'], lw['w_cq'])


def _sample_mem_attn_kernel(q_ref, mkt_ref, mvt_ref, o_ref, *, bb):
    dm, n_mem = mkt_ref.shape[1:]
    heads = range(MEM_HEADS)
    rows_of = lambda h: slice(h * MEM_HEAD_DIM, (h + 1) * MEM_HEAD_DIM)

    def body(r, carry):
        q = q_ref[pl.ds(r, 1), :]
        prod = mkt_ref[r].astype(BF16).astype(F32) * jnp.broadcast_to(q, (n_mem, dm)).T
        lg = jnp.concatenate([jnp.sum(prod[rows_of(h), :], axis=0, keepdims=True) for h in heads], axis=0)
        e = jnp.exp(lg - jnp.max(lg, axis=1, keepdims=True))
        pb = (e * (1.0 / jnp.sum(e, axis=1, keepdims=True))).astype(BF16).astype(F32)
        pe = jnp.concatenate([jnp.broadcast_to(pb[h:h + 1, :], (MEM_HEAD_DIM, n_mem)) for h in heads], axis=0)
        col = jnp.sum(pe * mvt_ref[r], axis=1, keepdims=True)
        o_ref[pl.ds(r, 1), :] = jnp.broadcast_to(col, (dm, LANES)).T[0:1, :]
        return carry

    lax.fori_loop(0, bb, body, 0)


def _sample_mem_attn(q, mk, mv, bb):
    db, dm, n_mem = mk.shape
    return pl.pallas_call(
        functools.partial(_sample_mem_attn_kernel, bb=bb), grid=(db // bb,),
        in_specs=[pl.BlockSpec((bb, dm), lambda i: (i, 0)),
                  pl.BlockSpec((bb, dm, n_mem), lambda i: (i, 0, 0)),
                  pl.BlockSpec((bb, dm, n_mem), lambda i: (i, 0, 0))],
        out_specs=pl.BlockSpec((bb, dm), lambda i: (i, 0)),
        out_shape=jax.ShapeDtypeStruct((db, dm), F32),
        compiler_params=_cparams(("parallel",)), name="sample_mem_attn",
    )(q, mk, mv)


def _sample_ffn_kernel(x1_ref, o_ref, h0_ref, h1_ref, wco_ref, gf_ref, wup_ref, fdw_ref, fdb_ref, wdn_ref,
                       gfin_ref, y_ref, up_ref, *, n_chunks):
    d_ff = wdn_ref.shape[0]
    hc = d_ff // n_chunks
    x2 = x1_ref[...] + _dot(o_ref[...].astype(BF16), wco_ref[...])
    hn = _rms(x2, gf_ref[...]).astype(BF16)

    def conv(lo):
        sl = slice(lo, lo + hc)
        up = _dot(hn, wup_ref[:, sl])
        up_ref[:, sl] = up
        return fdb_ref[:, sl] + fdw_ref[0:1, sl] * h0_ref[:, sl] + fdw_ref[1:2, sl] * h1_ref[:, sl] + fdw_ref[2:3, sl] * up

    acc = jnp.zeros(x1_ref.shape, F32)
    for cc in range(n_chunks):
        acc = acc + _dot(_ffn_act(conv(cc * hc), conv(d_ff + cc * hc)), wdn_ref[cc * hc:(cc + 1) * hc, :])
    y_ref[...] = _rms(x2 + acc, gfin_ref[...])


def _sample_ffn(x1, o, h0, h1, lw, gfin):
    d_ff = lw['w_down'].shape[0]
    return pl.pallas_call(
        functools.partial(_sample_ffn_kernel, n_chunks=2),
        out_shape=(jax.ShapeDtypeStruct(x1.shape, F32), jax.ShapeDtypeStruct((x1.shape[0], 2 * d_ff), F32)),
        compiler_params=pltpu.CompilerParams(vmem_limit_bytes=VMEM_LIMIT_BYTES), name="sample_ffn",
    )(x1, o, h0, h1, lw['w_co'], lw['norm_ffn'], lw['w_up'], lw['ffn_dw'], lw['ffn_db'], lw['w_down'], gfin)


def _prompt_layer(x, mem, rel_table, lw, gfin):
    b, s, _ = x.shape
    kt, vt, kit, kiwi, q_bf, k_bf, vt_bf, qi_bf, ki_bf, u = _inproj(x, lw['norm_mix'], lw['w_cat'], tm=512)
    c_bf = _conv_prompt(u, lw['conv_dw'], lw['conv_db'], lw['conv_ln_g'], lw['conv_ln_b'], tm=512)
    a_bf = _dsa_prompt(rel_table, q_bf, qi_bf, kiwi, ki_bf, k_bf, vt_bf, tq=256)
    mk, mv, mk_bf, mv_bf = _mem_kv(mem, lw['norm_mem_kv'], lw['w_ckv'])
    y, ffn_state = _post_prompt(x, a_bf, c_bf, mk_bf, mv_bf, lw, gfin, tm=256)
    heads_last = lambda t: jnp.transpose(t.reshape(b, N_HEADS, HEAD_DIM, s), (0, 3, 1, 2))
    return (y, heads_last(kt), heads_last(vt), jnp.swapaxes(kit, 1, 2), u[:, s - (CONV_WIDTH - 1):],
            ffn_state, mk, mv)


def _sample_layer(x, cache_k, cache_v, cache_idx_k, page_table, cache_conv, cache_ffn, mem_k, mem_v,
                  rel_table, lw, gfin):
    db = x.shape[0]
    n_pages = page_table.shape[1]
    ps = cache_k.shape[1]
    xs = x.reshape(1, db, -1)
    kt, vt, _, kiwi, q_bf, _, _, qi_bf, _, u = (t[0] for t in _inproj(xs, lw['norm_mix'], lw['w_cat'], tm=db))
    k, v = kt.T, vt.T
    c_bf = _conv_sample(jnp.swapaxes(cache_conv, 0, 1), u, lw['conv_dw'], lw['conv_db'],
                        lw['conv_ln_g'], lw['conv_ln_b'])
    ki_new = kiwi[:, :IDX_DIM]
    w_idx = kiwi[:, IDX_DIM:IDX_DIM + N_IDX_HEADS]
    pages_t = lambda c: jnp.moveaxis(c.reshape(c.shape[0], ps, -1), 1, 2)
    scores, snew = _dsa_sample_scores(
        page_table, qi_bf.astype(F32).reshape(db, N_IDX_HEADS, IDX_DIM), w_idx.reshape(db, N_IDX_HEADS, 1),
        ki_new.reshape(db, 1, IDX_DIM), pages_t(cache_idx_k), rows=4)
    sel, sel_new = _dsa_sample_select(scores.reshape(db, n_pages * ps), snew.reshape(db, LANES))
    a = _dsa_sample_attend(
        page_table, rel_table, q_bf.astype(F32).reshape(db, 1, D_ATTN), k.reshape(db, 1, D_ATTN),
        v.reshape(db, 1, D_ATTN), sel.reshape(db, 1, n_pages * ps), sel_new.reshape(db, 1, LANES),
        pages_t(cache_k), pages_t(cache_v)).reshape(db, D_ATTN)
    x1, q_mem = _sample_outproj(x.reshape(db, -1), a, c_bf, lw)
    mem_t = lambda m: jnp.moveaxis(m.reshape(db, -1, D_MEM_ATTN), 1, 2)
    o = _sample_mem_attn(q_mem, mem_t(mem_k), mem_t(mem_v), bb=16)
    y, up = _sample_ffn(x1, o, cache_ffn[:, 0], cache_ffn[:, 1], lw, gfin)
    conv_state = jnp.concatenate([cache_conv[:, 1:], u[:, None, :]], axis=1)
    ffn_state = jnp.concatenate([cache_ffn[:, 1:], up[:, None, :]], axis=1)
    return y, k, v, ki_new, conv_state, ffn_state


def _layer_weights(l, norm_mix, w_in, conv_dw, conv_db, conv_ln_g, conv_ln_b, w_o, norm_mem_q, norm_mem_kv,
                   w_cq, w_ckv, w_co, norm_ffn, w_up, ffn_dw, ffn_db, w_down):
    n_main = 3 * D_ATTN + N_IDX_HEADS * IDX_DIM
    n_small = IDX_DIM + N_IDX_HEADS
    w = w_in[l]
    w_cat = jnp.concatenate(
        [w[:, :n_main], w[:, n_main + n_small:], w[:, n_main:n_main + n_small],
         jnp.zeros((w.shape[0], LANES - n_small), w.dtype)], axis=1).astype(BF16)
    row = lambda a: a[l].reshape(1, -1)
    return dict(
        norm_mix=row(norm_mix), w_cat=w_cat, conv_dw=conv_dw[l], conv_db=row(conv_db),
        conv_ln_g=row(conv_ln_g), conv_ln_b=row(conv_ln_b), w_o=w_o[l].astype(BF16),
        norm_mem_q=row(norm_mem_q), norm_mem_kv=row(norm_mem_kv), w_cq=w_cq[l].astype(BF16),
        w_ckv=w_ckv[l].astype(BF16), w_co=w_co[l].astype(BF16), norm_ffn=row(norm_ffn),
        w_up=w_up[l].astype(BF16), ffn_dw=ffn_dw[l], ffn_db=row(ffn_db), w_down=w_down[l].astype(BF16))


def kernel(x_prompt, x_sample, mem_prompt, cache_k, cache_v, cache_idx_k, page_table, cache_conv, cache_ffn,
           cache_mem_k, cache_mem_v, rel_bias, norm_mix, w_in, conv_dw, conv_db, conv_ln_g, conv_ln_b, w_o,
           norm_mem_q, norm_mem_kv, w_cq, w_ckv, w_co, norm_ffn, w_up, ffn_dw, ffn_db, w_down, norm_final):
    depth = w_in.shape[0]
    b, s, d = x_prompt.shape
    db = x_sample.shape[0]
    assert x_sample.shape[1] == 1
    yp, ys = x_prompt, x_sample
    outs = [[] for _ in range(12)]
    ones = jnp.ones((1, d), F32)
    for l in range(depth):
        lw = _layer_weights(l, norm_mix, w_in, conv_dw, conv_db, conv_ln_g, conv_ln_b, w_o, norm_mem_q,
                            norm_mem_kv, w_cq, w_ckv, w_co, norm_ffn, w_up, ffn_dw, ffn_db, w_down)
        assert depth == 1, "final-norm fusion below assumes a single layer"
        gfin = norm_final.reshape(1, d) if l == depth - 1 else ones
        yp, kp, vp, kip, cp, fp, mkp, mvp = _prompt_layer(yp, mem_prompt, rel_bias, lw, gfin)
        ys, k_s, v_s, ki_s, c_s, f_s = _sample_layer(
            ys, cache_k[l], cache_v[l], cache_idx_k[l], page_table, cache_conv[l], cache_ffn[l],
            cache_mem_k[l], cache_mem_v[l], rel_bias, lw, gfin)
        ys = ys.reshape(db, 1, d)
        vals = (kp.reshape(b, s, N_HEADS, HEAD_DIM), vp.reshape(b, s, N_HEADS, HEAD_DIM), kip,
                k_s.reshape(db, 1, N_HEADS, HEAD_DIM), v_s.reshape(db, 1, N_HEADS, HEAD_DIM),
                ki_s.reshape(db, 1, IDX_DIM), cp, c_s, fp, f_s,
                mkp.reshape(b, -1, MEM_HEADS, MEM_HEAD_DIM), mvp.reshape(b, -1, MEM_HEADS, MEM_HEAD_DIM))
        for lst, val in zip(outs, vals):
            lst.append(val)
    return (yp, ys) + tuple(jnp.stack(lst) for lst in outs)
```
